```python
import math
import jax, jax.numpy as jnp
from jax import lax
import numpy as np

D_MODEL = 4096
BATCH = 4
SEQ = 2048
DEPTH = 4
DEC_BATCH = 8
DEC_SEQ = 1
PAST_LEN = 8192
PAGE_SIZE = 128

D_A = D_MODEL // 4
D_B = D_MODEL // 4
D_C = D_MODEL // 2
HEAD_DIM = 128
N_HEADS = D_C // HEAD_DIM
KV_HEADS = 4
GQ = N_HEADS // KV_HEADS
KVD = KV_HEADS * HEAD_DIM
CONV_WIDTH = 31
SSM_GROUP = 16
SSM_GROUPS = D_B // SSM_GROUP
SSM_STATE = 64
CMP_BLOCK = 32
CMP_STRIDE = 16
SLC_BLOCK = 64
TOP_K = 16
WINDOW = 512
WIN_QB = 128
SLC_Q_CHUNK = 32
ROPE_DIM = HEAD_DIM // 4
ROPE_THETA = 500000.0
D_FF = ((8 * D_MODEL // 3 + 255) // 256) * 256
EPS = 1e-6
SCALE = HEAD_DIM ** -0.5
NEG = -1e30
FORCE = 1e4
IN_SIZES = (2 * D_A, D_B, D_C, KVD, KVD, KVD, KVD, KVD, KVD, 3 * N_HEADS)
D_IN = 2 * D_A + D_B + D_C + 6 * KVD + 3 * N_HEADS

kernel_name = 'hybrid_conv_s5_nsa_decoder_step'


def rmsnorm(x, g):
    xf = x.astype(jnp.float32)
    y = xf * lax.rsqrt(jnp.mean(xf * xf, axis=-1, keepdims=True) + EPS)
    return (y * g.astype(jnp.float32)).astype(x.dtype)


def layernorm(x, g, b):
    xf = x.astype(jnp.float32)
    mu = jnp.mean(xf, axis=-1, keepdims=True)
    var = jnp.mean(jnp.square(xf - mu), axis=-1, keepdims=True)
    y = (xf - mu) * lax.rsqrt(var + EPS)
    return (y * g.astype(jnp.float32) + b.astype(jnp.float32)).astype(x.dtype)


def swiglu_ffn(x, g, w_gate, w_up, w_down):
    h = rmsnorm(x, g)
    return (jax.nn.silu(h @ w_gate) * (h @ w_up)) @ w_down


def partial_rope(x, pos):
    half = ROPE_DIM // 2
    inv_freq = ROPE_THETA ** (-jnp.arange(half, dtype=jnp.float32) / half)
    ang = pos.astype(jnp.float32)[:, None] * inv_freq[None, :]
    cos = jnp.cos(ang)[None, :, None, :]
    sin = jnp.sin(ang)[None, :, None, :]
    xr = x[..., :ROPE_DIM].astype(jnp.float32)
    x1, x2 = xr[..., :half], xr[..., half:]
    rot = jnp.concatenate([x1 * cos - x2 * sin, x2 * cos + x1 * sin], axis=-1)
    return jnp.concatenate([rot.astype(x.dtype), x[..., ROPE_DIM:]], axis=-1)


def masked_softmax(s, mask):
    s = jnp.where(mask, s, NEG)
    m = jnp.max(s, axis=-1, keepdims=True)
    e = jnp.where(mask, jnp.exp(s - m), 0.0)
    return e / jnp.maximum(jnp.sum(e, axis=-1, keepdims=True), 1e-30)


def gqa_attend(q, k, v, mask):
    s = jnp.einsum('...qkgd,...skd->...qkgs', q, k).astype(jnp.float32) * SCALE
    p = masked_softmax(s, mask)
    return jnp.einsum('...qkgs,...skd->...qkgd', p.astype(v.dtype), v), p


def split_columns(z):
    offs = []
    acc = 0
    for n in IN_SIZES[:-1]:
        acc += n
        offs.append(acc)
    return jnp.split(z, offs, axis=-1)


def gather_pages(pool, page_table):
    g = pool[page_table]
    return g.reshape(g.shape[0], g.shape[1] * g.shape[2], *g.shape[3:])


def causal_depthwise_conv(xpad, w):
    return lax.conv_general_dilated(xpad, w[:, None, :].astype(xpad.dtype), window_strides=(1,), padding='VALID', dimension_numbers=('NWC', 'WIO', 'NWC'), feature_group_count=xpad.shape[-1])


def conformer_conv(z_a, buf, dw, dw_b, ln_g, ln_b, w_proj):
    a = z_a[..., :D_A] * jax.nn.sigmoid(z_a[..., D_A:])
    xpad = jnp.concatenate([buf.astype(a.dtype), a], axis=1)
    y = causal_depthwise_conv(xpad, dw) + dw_b
    y = jax.nn.silu(layernorm(y, ln_g, ln_b))
    return y @ w_proj, xpad[:, -(CONV_WIDTH - 1):]


def _scan_combine(e1, e2):
    a1, b1 = e1
    a2, b2 = e2
    return a2 * a1, a2 * b1 + b2


def s5_mixer(xb, s0_re, s0_im, lam_re, lam_im, log_dt, b_re, b_im, c_re, c_im, d, w_glu, b_glu, w_proj):
    B_, T, _ = xb.shape
    xf = xb.astype(jnp.float32)
    u = xf.reshape(B_, T, SSM_GROUPS, SSM_GROUP).astype(jnp.complex64)
    lam = lax.complex(lam_re.astype(jnp.float32), lam_im.astype(jnp.float32))
    dt = jnp.exp(log_dt.astype(jnp.float32))[:, None]
    lam_bar = jnp.exp(lam * dt)
    bmat = lax.complex(b_re.astype(jnp.float32), b_im.astype(jnp.float32))
    b_bar = ((lam_bar - 1.0) / lam)[..., None] * bmat
    bu = jnp.einsum('btgc,gpc->btgp', u, b_bar)
    a = jnp.broadcast_to(lam_bar, bu.shape)
    a_cum, s = lax.associative_scan(_scan_combine, (a, bu), axis=1)
    s0 = lax.complex(s0_re.astype(jnp.float32), s0_im.astype(jnp.float32))
    s = s + a_cum * s0[:, None]
    cmat = lax.complex(c_re.astype(jnp.float32), c_im.astype(jnp.float32))
    y = jnp.real(jnp.einsum('btgp,gcp->btgc', s, cmat)).reshape(B_, T, D_B) + d * xf
    y = jax.nn.gelu(y)
    y = y * jax.nn.sigmoid(y @ w_glu + b_glu)
    s_last = s[:, -1]
    return y.astype(xb.dtype) @ w_proj, jnp.real(s_last), jnp.imag(s_last)


def select_attend(q, sel, ok, pos, kb, vb):
    bi = jnp.arange(q.shape[0])[:, None, None, None]
    hi = jnp.arange(KV_HEADS)[None, None, :, None]
    kg = kb[bi, hi, sel]
    vg = vb[bi, hi, sel]
    s = jnp.einsum('btkgd,btknsd->btkgns', q, kg).astype(jnp.float32) * SCALE
    tok = sel[..., None] * SLC_BLOCK + jnp.arange(SLC_BLOCK, dtype=sel.dtype)
    mask = (tok <= pos[None, :, None, None, None]) & ok[..., None]
    shp = s.shape
    p = masked_softmax(s.reshape(*shp[:4], -1), mask.reshape(*mask.shape[:3], 1, -1)).reshape(shp)
    return jnp.einsum('btkgns,btknsd->btkgd', p.astype(vg.dtype), vg)


def nsa_compress_select(qn, qr, kc_rows, vc_rows, ks_rows, vs_rows, pos, cmp_wk, cmp_wv, k_norm_cmp):
    B_, L = kc_rows.shape[:2]
    T = qn.shape[1]
    n_cmp = (L - CMP_BLOCK) // CMP_STRIDE + 1
    starts = jnp.arange(n_cmp, dtype=jnp.int32) * CMP_STRIDE
    idx = starts[:, None] + jnp.arange(CMP_BLOCK, dtype=jnp.int32)[None, :]
    k_cmp = jnp.einsum('bjikd,kid->bjkd', kc_rows[:, idx], cmp_wk)
    v_cmp = jnp.einsum('bjikd,kid->bjkd', vc_rows[:, idx], cmp_wv)
    k_cmp = rmsnorm(k_cmp, k_norm_cmp)
    mask_c = (starts + CMP_BLOCK - 1)[None, :] <= pos[:, None]
    o_cmp, p_cmp = gqa_attend(qn, k_cmp, v_cmp, mask_c[None, :, None, None, :])
    n_slc = -(-L // SLC_BLOCK)
    blk = jnp.arange(n_slc, dtype=jnp.int32)
    blk_start = blk * SLC_BLOCK
    overlap = ((starts[:, None] < blk_start[None, :] + SLC_BLOCK) & (starts[:, None] + CMP_BLOCK > blk_start[None, :])).astype(jnp.float32)
    imp = jnp.einsum('btkgj,jn->btkn', p_cmp, overlap)
    cur = pos // SLC_BLOCK
    forced = (blk[None, :] == 0) | (blk[None, :] == cur[:, None]) | (blk[None, :] == cur[:, None] - 1)
    valid = blk[None, :] <= cur[:, None]
    score = jnp.where(forced[None, :, None, :], FORCE, imp)
    score = jnp.where(valid[None, :, None, :], score, -FORCE)
    _, sel = lax.top_k(score, min(TOP_K, n_slc))
    sel_ok = sel <= cur[None, :, None, None]
    pad = n_slc * SLC_BLOCK - L
    ks_b = jnp.pad(ks_rows, ((0, 0), (0, pad), (0, 0), (0, 0))).reshape(B_, n_slc, SLC_BLOCK, KV_HEADS, HEAD_DIM).transpose(0, 3, 1, 2, 4)
    vs_b = jnp.pad(vs_rows, ((0, 0), (0, pad), (0, 0), (0, 0))).reshape(B_, n_slc, SLC_BLOCK, KV_HEADS, HEAD_DIM).transpose(0, 3, 1, 2, 4)
    chunk = SLC_Q_CHUNK if T % SLC_Q_CHUNK == 0 else T
    n_chunks = T // chunk

    def to_chunks(a):
        return a.reshape(a.shape[0], n_chunks, chunk, *a.shape[2:]).swapaxes(0, 1)

    def attend_chunk(args):
        q_c, sel_c, ok_c, pos_c = args
        return select_attend(q_c, sel_c, ok_c, pos_c, ks_b, vs_b)

    o = lax.map(attend_chunk, (to_chunks(qr), to_chunks(sel), to_chunks(sel_ok), pos.reshape(n_chunks, chunk)))
    o_slc = o.swapaxes(0, 1).reshape(qr.shape)
    return o_cmp, o_slc


def band_window(qr, kw, vw):
    B_, T = qr.shape[:2]
    nb = T // WIN_QB
    span = WINDOW + WIN_QB
    kpad = jnp.pad(kw, ((0, 0), (WINDOW, 0), (0, 0), (0, 0)))
    vpad = jnp.pad(vw, ((0, 0), (WINDOW, 0), (0, 0), (0, 0)))
    idx = jnp.arange(nb, dtype=jnp.int32)[:, None] * WIN_QB + jnp.arange(span, dtype=jnp.int32)[None, :]
    qpos = jnp.arange(nb, dtype=jnp.int32)[:, None] * WIN_QB + jnp.arange(WIN_QB, dtype=jnp.int32)[None, :]
    kpos = idx - WINDOW
    diff = qpos[:, :, None] - kpos[:, None, :]
    mask = (diff >= 0) & (diff <= WINDOW) & (kpos[:, None, :] >= 0)
    o, _ = gqa_attend(qr.reshape(B_, nb, WIN_QB, KV_HEADS, GQ, HEAD_DIM), kpad[:, idx], vpad[:, idx], mask[None, :, :, None, None, :])
    return o.reshape(qr.shape)


def nsa_mixer(z_q, z_kc, z_vc, z_ks, z_vs, z_kw, z_vw, z_g, pos, past, lw):
    B_, T = z_q.shape[:2]
    q = z_q.reshape(B_, T, N_HEADS, HEAD_DIM)
    qn = rmsnorm(q, lw['nsa_q_norm'])
    qr = partial_rope(qn, pos).reshape(B_, T, KV_HEADS, GQ, HEAD_DIM)
    qn = qn.reshape(B_, T, KV_HEADS, GQ, HEAD_DIM)
    kvshape = (B_, T, KV_HEADS, HEAD_DIM)
    kc_new = z_kc.reshape(kvshape)
    vc_new = z_vc.reshape(kvshape)
    ks_new = partial_rope(rmsnorm(z_ks.reshape(kvshape), lw['nsa_k_norm'][1]), pos)
    vs_new = z_vs.reshape(kvshape)
    kw_new = partial_rope(rmsnorm(z_kw.reshape(kvshape), lw['nsa_k_norm'][2]), pos)
    vw_new = z_vw.reshape(kvshape)
    if past is None:
        kc_rows, vc_rows, ks_rows, vs_rows = kc_new, vc_new, ks_new, vs_new
        o_win = band_window(qr, kw_new, vw_new)
        keep = min(WINDOW, T)
        win_k_new = kw_new[:, -keep:]
        win_v_new = vw_new[:, -keep:]
    else:
        kc_rows = jnp.concatenate([past['kc'], kc_new], axis=1)
        vc_rows = jnp.concatenate([past['vc'], vc_new], axis=1)
        ks_rows = jnp.concatenate([past['ks'], ks_new], axis=1)
        vs_rows = jnp.concatenate([past['vs'], vs_new], axis=1)
        wb = past['win_k'].shape[1]
        kw_all = jnp.concatenate([past['win_k'], kw_new], axis=1)
        vw_all = jnp.concatenate([past['win_v'], vw_new], axis=1)
        kpos = pos[0] - wb + jnp.arange(wb + T, dtype=jnp.int32)
        diff = pos[:, None] - kpos[None, :]
        mask = (diff >= 0) & (diff <= WINDOW)
        o_win, _ = gqa_attend(qr, kw_all, vw_all, mask[None, :, None, None, :])
        win_k_new = kw_all[:, -wb:]
        win_v_new = vw_all[:, -wb:]
    o_cmp, o_slc = nsa_compress_select(qn, qr, kc_rows, vc_rows, ks_rows, vs_rows, pos, lw['nsa_cmp_wk'], lw['nsa_cmp_wv'], lw['nsa_k_norm'][0])
    g = jax.nn.sigmoid(z_g.astype(jnp.float32)).reshape(B_, T, KV_HEADS, GQ, 3).astype(z_q.dtype)
    o = g[..., 0:1] * o_cmp + g[..., 1:2] * o_slc + g[..., 2:3] * o_win
    o = o.reshape(B_, T, D_C) @ lw['nsa_w_out']
    return o, (kc_new, vc_new, ks_new, vs_new, win_k_new, win_v_new)


def layer_forward(x, pos, past, lw):
    B_, T, _ = x.shape
    h = x + 0.5 * swiglu_ffn(x, lw['ffn1_norm'], lw['ffn1_w_gate'], lw['ffn1_w_up'], lw['ffn1_w_down'])
    u = rmsnorm(h, lw['mix_norm'])
    z_a, z_b, z_q, z_kc, z_vc, z_ks, z_vs, z_kw, z_vw, z_g = split_columns(u @ lw['w_in'])
    if past is None:
        conv_buf = jnp.zeros((B_, CONV_WIDTH - 1, D_A), x.dtype)
        s0_re = jnp.zeros((B_, SSM_GROUPS, SSM_STATE), jnp.float32)
        s0_im = jnp.zeros((B_, SSM_GROUPS, SSM_STATE), jnp.float32)
    else:
        conv_buf, s0_re, s0_im = past['conv'], past['s_re'], past['s_im']
    o_a, conv_new = conformer_conv(z_a, conv_buf, lw['conv_dw'], lw['conv_dw_bias'], lw['conv_ln_g'], lw['conv_ln_b'], lw['conv_w_out'])
    o_b, s_re, s_im = s5_mixer(z_b, s0_re, s0_im, lw['ssm_lambda_re'], lw['ssm_lambda_im'], lw['ssm_log_dt'], lw['ssm_b_re'], lw['ssm_b_im'], lw['ssm_c_re'], lw['ssm_c_im'], lw['ssm_d'], lw['ssm_w_glu'], lw['ssm_b_glu'], lw['ssm_w_out'])
    o_c, kv_new = nsa_mixer(z_q, z_kc, z_vc, z_ks, z_vs, z_kw, z_vw, z_g, pos, past, lw)
    gates = jax.nn.sigmoid((u @ lw['merge_w_gate'] + lw['merge_b_gate']).astype(jnp.float32))
    gates = gates.reshape(B_, T, 3, D_MODEL).astype(x.dtype)
    mixed = gates[:, :, 0] * o_a + gates[:, :, 1] * o_b + gates[:, :, 2] * o_c
    h = h + mixed @ lw['w_out']
    y = h + 0.5 * swiglu_ffn(h, lw['ffn2_norm'], lw['ffn2_w_gate'], lw['ffn2_w_up'], lw['ffn2_w_down'])
    return y, (kv_new[0], kv_new[1], kv_new[2], kv_new[3], kv_new[4], kv_new[5], conv_new, s_re, s_im)


def setup_inputs(seed: int = 0) -> dict:
    key = jax.random.key(seed)
    keys = iter(jax.random.split(key, 64))
    f32 = jnp.float32

    def nrm(shape, scale):
        return jax.random.normal(next(keys), shape, f32) * scale

    def gain(shape):
        return 1.0 + 0.05 * jax.random.normal(next(keys), shape, f32)

    n_pages = PAST_LEN // PAGE_SIZE
    n_used = DEC_BATCH * n_pages
    n_pool = n_used + max(1, n_used // 4)
    wb = min(WINDOW, PAST_LEN)
    pool_shape = (DEPTH, n_pool, PAGE_SIZE, KV_HEADS, HEAD_DIM)
    page_table = jax.random.permutation(next(keys), n_pool)[:n_used].astype(jnp.int32).reshape(DEC_BATCH, n_pages)
    lam_im = jnp.pi * jnp.arange(SSM_STATE, dtype=f32)[None, None, :] + 0.01 * jax.random.normal(next(keys), (DEPTH, SSM_GROUPS, SSM_STATE), f32)
    log_dt = jax.random.uniform(next(keys), (DEPTH, SSM_GROUPS), f32, math.log(1e-3), math.log(1e-1))
    return {
        'x_prompt': nrm((BATCH, SEQ, D_MODEL), 1.0),
        'x_sample': nrm((DEC_BATCH, DEC_SEQ, D_MODEL), 1.0),
        'cache_cmp_k': nrm(pool_shape, 1.0),
        'cache_cmp_v': nrm(pool_shape, 1.0),
        'cache_slc_k': nrm(pool_shape, 1.0),
        'cache_slc_v': nrm(pool_shape, 1.0),
        'cache_win_k': nrm((DEPTH, DEC_BATCH, wb, KV_HEADS, HEAD_DIM), 1.0),
        'cache_win_v': nrm((DEPTH, DEC_BATCH, wb, KV_HEADS, HEAD_DIM), 1.0),
        'state_conv': nrm((DEPTH, DEC_BATCH, CONV_WIDTH - 1, D_A), 0.5),
        'state_ssm_re': nrm((DEPTH, DEC_BATCH, SSM_GROUPS, SSM_STATE), 0.3),
        'state_ssm_im': nrm((DEPTH, DEC_BATCH, SSM_GROUPS, SSM_STATE), 0.3),
        'page_table': page_table,
        'ffn1_norm': gain((DEPTH, D_MODEL)),
        'ffn1_w_gate': nrm((DEPTH, D_MODEL, D_FF), D_MODEL ** -0.5),
        'ffn1_w_up': nrm((DEPTH, D_MODEL, D_FF), D_MODEL ** -0.5),
        'ffn1_w_down': nrm((DEPTH, D_FF, D_MODEL), D_FF ** -0.5),
        'mix_norm': gain((DEPTH, D_MODEL)),
        'w_in': nrm((DEPTH, D_MODEL, D_IN), D_MODEL ** -0.5),
        'conv_dw': nrm((DEPTH, CONV_WIDTH, D_A), CONV_WIDTH ** -0.5),
        'conv_dw_bias': nrm((DEPTH, D_A), 0.02),
        'conv_ln_g': gain((DEPTH, D_A)),
        'conv_ln_b': nrm((DEPTH, D_A), 0.02),
        'conv_w_out': nrm((DEPTH, D_A, D_MODEL), D_A ** -0.5),
        'ssm_lambda_re': -0.5 + 0.01 * jax.random.normal(next(keys), (DEPTH, SSM_GROUPS, SSM_STATE), f32),
        'ssm_lambda_im': lam_im,
        'ssm_log_dt': log_dt,
        'ssm_b_re': nrm((DEPTH, SSM_GROUPS, SSM_STATE, SSM_GROUP), (2 * SSM_GROUP) ** -0.5),
        'ssm_b_im': nrm((DEPTH, SSM_GROUPS, SSM_STATE, SSM_GROUP), (2 * SSM_GROUP) ** -0.5),
        'ssm_c_re': nrm((DEPTH, SSM_GROUPS, SSM_GROUP, SSM_STATE), SSM_STATE ** -0.5),
        'ssm_c_im': nrm((DEPTH, SSM_GROUPS, SSM_GROUP, SSM_STATE), SSM_STATE ** -0.5),
        'ssm_d': nrm((DEPTH, D_B), 1.0),
        'ssm_w_glu': nrm((DEPTH, D_B, D_B), D_B ** -0.5),
        'ssm_b_glu': nrm((DEPTH, D_B), 0.02),
        'ssm_w_out': nrm((DEPTH, D_B, D_MODEL), D_B ** -0.5),
        'nsa_q_norm': gain((DEPTH, HEAD_DIM)),
        'nsa_k_norm': gain((DEPTH, 3, HEAD_DIM)),
        'nsa_cmp_wk': (1.0 + 0.1 * jax.random.normal(next(keys), (DEPTH, KV_HEADS, CMP_BLOCK, HEAD_DIM), f32)) / CMP_BLOCK,
        'nsa_cmp_wv': (1.0 + 0.1 * jax.random.normal(next(keys), (DEPTH, KV_HEADS, CMP_BLOCK, HEAD_DIM), f32)) / CMP_BLOCK,
        'nsa_w_out': nrm((DEPTH, D_C, D_MODEL), D_C ** -0.5),
        'merge_w_gate': nrm((DEPTH, D_MODEL, 3 * D_MODEL), D_MODEL ** -0.5),
        'merge_b_gate': nrm((DEPTH, 3 * D_MODEL), 0.02),
        'w_out': nrm((DEPTH, D_MODEL, D_MODEL), D_MODEL ** -0.5),
        'ffn2_norm': gain((DEPTH, D_MODEL)),
        'ffn2_w_gate': nrm((DEPTH, D_MODEL, D_FF), D_MODEL ** -0.5),
        'ffn2_w_up': nrm((DEPTH, D_MODEL, D_FF), D_MODEL ** -0.5),
        'ffn2_w_down': nrm((DEPTH, D_FF, D_MODEL), D_FF ** -0.5),
    }


def reference(x_prompt, x_sample, cache_cmp_k, cache_cmp_v, cache_slc_k, cache_slc_v, cache_win_k, cache_win_v, state_conv, state_ssm_re, state_ssm_im, page_table, ffn1_norm, ffn1_w_gate, ffn1_w_up, ffn1_w_down, mix_norm, w_in, conv_dw, conv_dw_bias, conv_ln_g, conv_ln_b, conv_w_out, ssm_lambda_re, ssm_lambda_im, ssm_log_dt, ssm_b_re, ssm_b_im, ssm_c_re, ssm_c_im, ssm_d, ssm_w_glu, ssm_b_glu, ssm_w_out, nsa_q_norm, nsa_k_norm, nsa_cmp_wk, nsa_cmp_wv, nsa_w_out, merge_w_gate, merge_b_gate, w_out, ffn2_norm, ffn2_w_gate, ffn2_w_up, ffn2_w_down):
    past_len = page_table.shape[1] * PAGE_SIZE
    pos_p = jnp.arange(x_prompt.shape[1], dtype=jnp.int32)
    pos_s = past_len + jnp.arange(x_sample.shape[1], dtype=jnp.int32)
    hp, hs = x_prompt, x_sample
    st_p, st_s = [], []
    for l in range(DEPTH):
        lw = dict(ffn1_norm=ffn1_norm[l], ffn1_w_gate=ffn1_w_gate[l], ffn1_w_up=ffn1_w_up[l], ffn1_w_down=ffn1_w_down[l],
                  mix_norm=mix_norm[l], w_in=w_in[l],
                  conv_dw=conv_dw[l], conv_dw_bias=conv_dw_bias[l], conv_ln_g=conv_ln_g[l], conv_ln_b=conv_ln_b[l], conv_w_out=conv_w_out[l],
                  ssm_lambda_re=ssm_lambda_re[l], ssm_lambda_im=ssm_lambda_im[l], ssm_log_dt=ssm_log_dt[l],
                  ssm_b_re=ssm_b_re[l], ssm_b_im=ssm_b_im[l], ssm_c_re=ssm_c_re[l], ssm_c_im=ssm_c_im[l],
                  ssm_d=ssm_d[l], ssm_w_glu=ssm_w_glu[l], ssm_b_glu=ssm_b_glu[l], ssm_w_out=ssm_w_out[l],
                  nsa_q_norm=nsa_q_norm[l], nsa_k_norm=nsa_k_norm[l], nsa_cmp_wk=nsa_cmp_wk[l], nsa_cmp_wv=nsa_cmp_wv[l], nsa_w_out=nsa_w_out[l],
                  merge_w_gate=merge_w_gate[l], merge_b_gate=merge_b_gate[l], w_out=w_out[l],
                  ffn2_norm=ffn2_norm[l], ffn2_w_gate=ffn2_w_gate[l], ffn2_w_up=ffn2_w_up[l], ffn2_w_down=ffn2_w_down[l])
        hp, new_p = layer_forward(hp, pos_p, None, lw)
        past = dict(kc=gather_pages(cache_cmp_k[l], page_table), vc=gather_pages(cache_cmp_v[l], page_table),
                    ks=gather_pages(cache_slc_k[l], page_table), vs=gather_pages(cache_slc_v[l], page_table),
                    win_k=cache_win_k[l], win_v=cache_win_v[l], conv=state_conv[l],
                    s_re=state_ssm_re[l], s_im=state_ssm_im[l])
        hs, new_s = layer_forward(hs, pos_s, past, lw)
        st_p.append(new_p)
        st_s.append(new_s)
    P = [jnp.stack([st[i] for st in st_p], axis=0) for i in range(9)]
    S = [jnp.stack([st[i] for st in st_s], axis=0) for i in range(9)]
    return (hp, hs, P[0], P[1], P[2], P[3], P[4], P[5], P[6], P[7], P[8], S[0], S[1], S[2], S[3], S[4], S[5], S[6], S[7], S[8])
```

```python
import functools
import math

import jax
import jax.numpy as jnp
from jax import lax
from jax.experimental import pallas as pl
from jax.experimental.pallas import tpu as pltpu

D_MODEL = 4096
DEPTH = 4
PAGE_SIZE = 128
D_A = D_MODEL // 4
D_B = D_MODEL // 4
D_C = D_MODEL // 2
HEAD_DIM = 128
N_HEADS = D_C // HEAD_DIM
KV_HEADS = 4
GQ = N_HEADS // KV_HEADS
KVD = KV_HEADS * HEAD_DIM
CONV_WIDTH = 31
SSM_GROUP = 16
SSM_GROUPS = D_B // SSM_GROUP
SSM_STATE = 64
CMP_BLOCK = 32
CMP_STRIDE = 16
SLC_BLOCK = 64
TOP_K = 16
WINDOW = 512
WIN_QB = 128
SLC_Q_CHUNK = 32
ROPE_DIM = HEAD_DIM // 4
ROPE_THETA = 500000.0
D_FF = ((8 * D_MODEL // 3 + 255) // 256) * 256
EPS = 1e-6
SCALE = HEAD_DIM ** -0.5
NEG = -1e30
FORCE = 1e4
D_MAIN = 2 * D_A + D_B + D_C + 6 * KVD
N_GATE = 3 * N_HEADS
LANE = 128

VMEM_LIMIT_BYTES = 56 * 1024 * 1024


def _cparams(*sem):
    return pltpu.CompilerParams(dimension_semantics=sem, vmem_limit_bytes=VMEM_LIMIT_BYTES)


def _rmsnorm_kernel(x_ref, g_ref, o_ref):
    x = x_ref[...]
    ms = jnp.mean(x * x, axis=-1, keepdims=True)
    o_ref[...] = (x * lax.rsqrt(ms + EPS) * g_ref[...]).astype(o_ref.dtype)


def rmsnorm_rows(x, g, out_dtype=jnp.bfloat16):
    m, d = x.shape
    tr = min(m, 256)
    return pl.pallas_call(
        _rmsnorm_kernel,
        grid=(m // tr,),
        in_specs=[pl.BlockSpec((tr, d), lambda i: (i, 0)), pl.BlockSpec((1, d), lambda i: (0, 0))],
        out_specs=pl.BlockSpec((tr, d), lambda i: (i, 0)),
        out_shape=jax.ShapeDtypeStruct((m, d), out_dtype),
        compiler_params=_cparams("parallel"),
        name="rmsnorm_rows",
    )(x, g.reshape(1, d))


def _swiglu_up_kernel(a_ref, wg_ref, wu_ref, o_ref):
    a = a_ref[...]
    hg = jnp.dot(a, wg_ref[...], preferred_element_type=jnp.float32)
    hu = jnp.dot(a, wu_ref[...], preferred_element_type=jnp.float32)
    o_ref[...] = (hg * jax.nn.sigmoid(hg) * hu).astype(o_ref.dtype)


def swiglu_up(a, wg, wu):
    m, k = a.shape
    f = wg.shape[1]
    tm = min(m, 2048)
    tn = 256
    return pl.pallas_call(
        _swiglu_up_kernel,
        grid=(m // tm, f // tn),
        in_specs=[pl.BlockSpec((tm, k), lambda i, j: (i, 0)),
                  pl.BlockSpec((k, tn), lambda i, j: (0, j)),
                  pl.BlockSpec((k, tn), lambda i, j: (0, j))],
        out_specs=pl.BlockSpec((tm, tn), lambda i, j: (i, j)),
        out_shape=jax.ShapeDtypeStruct((m, f), jnp.bfloat16),
        compiler_params=_cparams("parallel", "arbitrary"),
        name="swiglu_up",
    )(a, wg, wu)


def _mm_res_kernel(a_ref, w_ref, r_ref, o_ref, acc_ref, *, nk, scale):
    kk = pl.program_id(2)
    part = jnp.dot(a_ref[...], w_ref[...], preferred_element_type=jnp.float32)

    @pl.when(kk == 0)
    def _():
        acc_ref[...] = part

    @pl.when(kk > 0)
    def _():
        acc_ref[...] += part

    @pl.when(kk == nk - 1)
    def _():
        o_ref[...] = r_ref[...] + scale * acc_ref[...]


def mm_residual(a, w, res, scale):
    m, k = a.shape
    n = w.shape[1]
    tm = min(m, 1024)
    tn = 512
    tk = k if k <= 4096 else k // 2
    nk = k // tk
    return pl.pallas_call(
        functools.partial(_mm_res_kernel, nk=nk, scale=scale),
        grid=(m // tm, n // tn, nk),
        in_specs=[pl.BlockSpec((tm, tk), lambda i, j, kk: (i, kk)),
                  pl.BlockSpec((tk, tn), lambda i, j, kk: (kk, j)),
                  pl.BlockSpec((tm, tn), lambda i, j, kk: (i, j))],
        out_specs=pl.BlockSpec((tm, tn), lambda i, j, kk: (i, j)),
        out_shape=jax.ShapeDtypeStruct((m, n), jnp.float32),
        scratch_shapes=[pltpu.VMEM((tm, tn), jnp.float32)],
        compiler_params=_cparams("parallel", "arbitrary", "arbitrary"),
        name="mm_residual",
    )(a, w, res)


def _mm_bias_kernel(a_ref, w_ref, b_ref, o_ref, *, act):
    y = jnp.dot(a_ref[...], w_ref[...], preferred_element_type=jnp.float32) + b_ref[...]
    if act == "sigmoid":
        y = jax.nn.sigmoid(y)
    o_ref[...] = y.astype(o_ref.dtype)


def mm_bias(a, w, bias, act=None, out_dtype=jnp.float32):
    m, k = a.shape
    n = w.shape[1]
    tm = min(m, 1024)
    tn = 512 if n % 512 == 0 else n
    return pl.pallas_call(
        functools.partial(_mm_bias_kernel, act=act),
        grid=(m // tm, n // tn),
        in_specs=[pl.BlockSpec((tm, k), lambda i, j: (i, 0)),
                  pl.BlockSpec((k, tn), lambda i, j: (0, j)),
                  pl.BlockSpec((1, tn), lambda i, j: (0, j))],
        out_specs=pl.BlockSpec((tm, tn), lambda i, j: (i, j)),
        out_shape=jax.ShapeDtypeStruct((m, n), out_dtype),
        compiler_params=_cparams("parallel", "arbitrary"),
        name="mm_bias",
    )(a, w, bias.reshape(1, n))


def _rmsnorm(x, g):
    xf = x.astype(jnp.float32)
    y = xf * lax.rsqrt(jnp.mean(xf * xf, axis=-1, keepdims=True) + EPS)
    return (y * g.astype(jnp.float32)).astype(x.dtype)


def _layernorm(x, g, b):
    xf = x.astype(jnp.float32)
    mu = jnp.mean(xf, axis=-1, keepdims=True)
    var = jnp.mean(jnp.square(xf - mu), axis=-1, keepdims=True)
    y = (xf - mu) * lax.rsqrt(var + EPS)
    return (y * g.astype(jnp.float32) + b.astype(jnp.float32)).astype(x.dtype)


def _partial_rope(x, pos):
    half = ROPE_DIM // 2
    inv_freq = ROPE_THETA ** (-jnp.arange(half, dtype=jnp.float32) / half)
    ang = pos.astype(jnp.float32)[:, None] * inv_freq[None, :]
    cos = jnp.cos(ang)[None, :, None, :]
    sin = jnp.sin(ang)[None, :, None, :]
    xr = x[..., :ROPE_DIM].astype(jnp.float32)
    x1, x2 = xr[..., :half], xr[..., half:]
    rot = jnp.concatenate([x1 * cos - x2 * sin, x2 * cos + x1 * sin], axis=-1)
    return jnp.concatenate([rot.astype(x.dtype), x[..., ROPE_DIM:]], axis=-1)


def _masked_softmax(s, mask):
    s = jnp.where(mask, s, NEG)
    m = jnp.max(s, axis=-1, keepdims=True)
    e = jnp.where(mask, jnp.exp(s - m), 0.0)
    return e / jnp.maximum(jnp.sum(e, axis=-1, keepdims=True), 1e-30)


def _gqa_attend(q, k, v, mask):
    s = jnp.einsum('...qkgd,...skd->...qkgs', q, k).astype(jnp.float32) * SCALE
    p = _masked_softmax(s, mask)
    return jnp.einsum('...qkgs,...skd->...qkgd', p.astype(v.dtype), v), p


def _gather_pages(pool, page_table):
    g = pool[page_table]
    return g.reshape(g.shape[0], g.shape[1] * g.shape[2], *g.shape[3:])


def _causal_depthwise_conv(xpad, w):
    return lax.conv_general_dilated(xpad, w[:, None, :].astype(xpad.dtype), window_strides=(1,), padding='VALID',
                                    dimension_numbers=('NWC', 'WIO', 'NWC'), feature_group_count=xpad.shape[-1])


def _conformer_conv(z_a, buf, dw, dw_b, ln_g, ln_b):
    a = z_a[..., :D_A] * jax.nn.sigmoid(z_a[..., D_A:])
    xpad = jnp.concatenate([buf.astype(a.dtype), a], axis=1)
    y = _causal_depthwise_conv(xpad, dw) + dw_b
    y = jax.nn.silu(_layernorm(y, ln_g, ln_b))
    return y, xpad[:, -(CONV_WIDTH - 1):]


def _scan_combine(e1, e2):
    a1, b1 = e1
    a2, b2 = e2
    return a2 * a1, a2 * b1 + b2


def _s5_mixer(xb, s0_re, s0_im, lw, wb):
    B_, T, _ = xb.shape
    xf = xb.astype(jnp.float32)
    u = xf.reshape(B_, T, SSM_GROUPS, SSM_GROUP).astype(jnp.complex64)
    lam = lax.complex(lw['ssm_lambda_re'], lw['ssm_lambda_im'])
    dt = jnp.exp(lw['ssm_log_dt'])[:, None]
    lam_bar = jnp.exp(lam * dt)
    bmat = lax.complex(lw['ssm_b_re'], lw['ssm_b_im'])
    b_bar = ((lam_bar - 1.0) / lam)[..., None] * bmat
    bu = jnp.einsum('btgc,gpc->btgp', u, b_bar)
    a = jnp.broadcast_to(lam_bar, bu.shape)
    a_cum, s = lax.associative_scan(_scan_combine, (a, bu), axis=1)
    s0 = lax.complex(s0_re, s0_im)
    s = s + a_cum * s0[:, None]
    cmat = lax.complex(lw['ssm_c_re'], lw['ssm_c_im'])
    y = jnp.real(jnp.einsum('btgp,gcp->btgc', s, cmat)).reshape(B_, T, D_B) + lw['ssm_d'] * xf
    y = jax.nn.gelu(y)
    yg = mm_bias(_pad_rows(y.reshape(B_ * T, D_B)).astype(jnp.bfloat16), wb['ssm_w_glu'], lw['ssm_b_glu'], act="sigmoid")
    y = y * yg[:B_ * T].reshape(B_, T, D_B)
    s_last = s[:, -1]
    return y, jnp.real(s_last), jnp.imag(s_last)


def _select_attend(q, sel, ok, pos, kb, vb):
    bi = jnp.arange(q.shape[0])[:, None, None, None]
    hi = jnp.arange(KV_HEADS)[None, None, :, None]
    kg = kb[bi, hi, sel]
    vg = vb[bi, hi, sel]
    s = jnp.einsum('btkgd,btknsd->btkgns', q, kg).astype(jnp.float32) * SCALE
    tok = sel[..., None] * SLC_BLOCK + jnp.arange(SLC_BLOCK, dtype=sel.dtype)
    mask = (tok <= pos[None, :, None, None, None]) & ok[..., None]
    shp = s.shape
    p = _masked_softmax(s.reshape(*shp[:4], -1), mask.reshape(*mask.shape[:3], 1, -1)).reshape(shp)
    return jnp.einsum('btkgns,btknsd->btkgd', p.astype(vg.dtype), vg)


def _nsa_compress_select(qn, qr, kc_rows, vc_rows, ks_rows, vs_rows, pos, cmp_wk, cmp_wv, k_norm_cmp):
    B_, L = kc_rows.shape[:2]
    T = qn.shape[1]
    n_cmp = (L - CMP_BLOCK) // CMP_STRIDE + 1
    starts = jnp.arange(n_cmp, dtype=jnp.int32) * CMP_STRIDE
    idx = starts[:, None] + jnp.arange(CMP_BLOCK, dtype=jnp.int32)[None, :]
    k_cmp = jnp.einsum('bjikd,kid->bjkd', kc_rows[:, idx], cmp_wk)
    v_cmp = jnp.einsum('bjikd,kid->bjkd', vc_rows[:, idx], cmp_wv)
    k_cmp = _rmsnorm(k_cmp, k_norm_cmp)
    mask_c = (starts + CMP_BLOCK - 1)[None, :] <= pos[:, None]
    o_cmp, p_cmp = _gqa_attend(qn, k_cmp, v_cmp, mask_c[None, :, None, None, :])
    n_slc = -(-L // SLC_BLOCK)
    blk = jnp.arange(n_slc, dtype=jnp.int32)
    blk_start = blk * SLC_BLOCK
    overlap = ((starts[:, None] < blk_start[None, :] + SLC_BLOCK) & (starts[:, None] + CMP_BLOCK > blk_start[None, :])).astype(jnp.float32)
    imp = jnp.einsum('btkgj,jn->btkn', p_cmp, overlap)
    cur = pos // SLC_BLOCK
    forced = (blk[None, :] == 0) | (blk[None, :] == cur[:, None]) | (blk[None, :] == cur[:, None] - 1)
    valid = blk[None, :] <= cur[:, None]
    score = jnp.where(forced[None, :, None, :], FORCE, imp)
    score = jnp.where(valid[None, :, None, :], score, -FORCE)
    _, sel = lax.top_k(score, min(TOP_K, n_slc))
    sel_ok = sel <= cur[None, :, None, None]
    pad = n_slc * SLC_BLOCK - L
    ks_b = jnp.pad(ks_rows, ((0, 0), (0, pad), (0, 0), (0, 0))).reshape(B_, n_slc, SLC_BLOCK, KV_HEADS, HEAD_DIM).transpose(0, 3, 1, 2, 4)
    vs_b = jnp.pad(vs_rows, ((0, 0), (0, pad), (0, 0), (0, 0))).reshape(B_, n_slc, SLC_BLOCK, KV_HEADS, HEAD_DIM).transpose(0, 3, 1, 2, 4)
    chunk = SLC_Q_CHUNK if T % SLC_Q_CHUNK == 0 else T
    n_chunks = T // chunk

    def to_chunks(a):
        return a.reshape(a.shape[0], n_chunks, chunk, *a.shape[2:]).swapaxes(0, 1)

    def attend_chunk(args):
        q_c, sel_c, ok_c, pos_c = args
        return _select_attend(q_c, sel_c, ok_c, pos_c, ks_b, vs_b)

    o = lax.map(attend_chunk, (to_chunks(qr), to_chunks(sel), to_chunks(sel_ok), pos.reshape(n_chunks, chunk)))
    o_slc = o.swapaxes(0, 1).reshape(qr.shape)
    return o_cmp, o_slc


def _band_window(qr, kw, vw):
    B_, T = qr.shape[:2]
    nb = T // WIN_QB
    span = WINDOW + WIN_QB
    kpad = jnp.pad(kw, ((0, 0), (WINDOW, 0), (0, 0), (0, 0)))
    vpad = jnp.pad(vw, ((0, 0), (WINDOW, 0), (0, 0), (0, 0)))
    idx = jnp.arange(nb, dtype=jnp.int32)[:, None] * WIN_QB + jnp.arange(span, dtype=jnp.int32)[None, :]
    qpos = jnp.arange(nb, dtype=jnp.int32)[:, None] * WIN_QB + jnp.arange(WIN_QB, dtype=jnp.int32)[None, :]
    kpos = idx - WINDOW
    diff = qpos[:, :, None] - kpos[:, None, :]
    mask = (diff >= 0) & (diff <= WINDOW) & (kpos[:, None, :] >= 0)
    o, _ = _gqa_attend(qr.reshape(B_, nb, WIN_QB, KV_HEADS, GQ, HEAD_DIM), kpad[:, idx], vpad[:, idx], mask[None, :, :, None, None, :])
    return o.reshape(qr.shape)


def _nsa_mixer(z_q, z_kc, z_vc, z_ks, z_vs, z_kw, z_vw, z_g, pos, past, lw):
    B_, T = z_q.shape[:2]
    q = z_q.reshape(B_, T, N_HEADS, HEAD_DIM)
    qn = _rmsnorm(q, lw['nsa_q_norm'])
    qr = _partial_rope(qn, pos).reshape(B_, T, KV_HEADS, GQ, HEAD_DIM)
    qn = qn.reshape(B_, T, KV_HEADS, GQ, HEAD_DIM)
    kvshape = (B_, T, KV_HEADS, HEAD_DIM)
    kc_new = z_kc.reshape(kvshape)
    vc_new = z_vc.reshape(kvshape)
    ks_new = _partial_rope(_rmsnorm(z_ks.reshape(kvshape), lw['nsa_k_norm'][1]), pos)
    vs_new = z_vs.reshape(kvshape)
    kw_new = _partial_rope(_rmsnorm(z_kw.reshape(kvshape), lw['nsa_k_norm'][2]), pos)
    vw_new = z_vw.reshape(kvshape)
    if past is None:
        kc_rows, vc_rows, ks_rows, vs_rows = kc_new, vc_new, ks_new, vs_new
        o_win = _band_window(qr, kw_new, vw_new)
        keep = min(WINDOW, T)
        win_k_new = kw_new[:, -keep:]
        win_v_new = vw_new[:, -keep:]
    else:
        kc_rows = jnp.concatenate([past['kc'], kc_new], axis=1)
        vc_rows = jnp.concatenate([past['vc'], vc_new], axis=1)
        ks_rows = jnp.concatenate([past['ks'], ks_new], axis=1)
        vs_rows = jnp.concatenate([past['vs'], vs_new], axis=1)
        wb = past['win_k'].shape[1]
        kw_all = jnp.concatenate([past['win_k'], kw_new], axis=1)
        vw_all = jnp.concatenate([past['win_v'], vw_new], axis=1)
        kpos = pos[0] - wb + jnp.arange(wb + T, dtype=jnp.int32)
        diff = pos[:, None] - kpos[None, :]
        mask = (diff >= 0) & (diff <= WINDOW)
        o_win, _ = _gqa_attend(qr, kw_all, vw_all, mask[None, :, None, None, :])
        win_k_new = kw_all[:, -wb:]
        win_v_new = vw_all[:, -wb:]
    o_cmp, o_slc = _nsa_compress_select(qn, qr, kc_rows, vc_rows, ks_rows, vs_rows, pos, lw['nsa_cmp_wk'], lw['nsa_cmp_wv'], lw['nsa_k_norm'][0])
    g = jax.nn.sigmoid(z_g.astype(jnp.float32)).reshape(B_, T, KV_HEADS, GQ, 3).astype(z_q.dtype)
    o = g[..., 0:1] * o_cmp + g[..., 1:2] * o_slc + g[..., 2:3] * o_win
    return o.reshape(B_, T, D_C), (kc_new, vc_new, ks_new, vs_new, win_k_new, win_v_new)


def _ffn(x2, g, wg, wu, wd):
    hn = rmsnorm_rows(x2, g)
    act = swiglu_up(hn, wg, wu)
    return mm_residual(act, wd, x2, 0.5)


def _pad_rows(x2, mult=16):
    m = x2.shape[0]
    mp = -(-m // mult) * mult
    return x2 if mp == m else jnp.pad(x2, ((0, mp - m), (0, 0)))


def _layer_forward(x, pos, past, lw, wb):
    B_, T, _ = x.shape
    m = B_ * T
    x2 = _pad_rows(x.reshape(m, D_MODEL))
    h2 = _ffn(x2, lw['ffn1_norm'], wb['ffn1_w_gate'], wb['ffn1_w_up'], wb['ffn1_w_down'])
    u2 = rmsnorm_rows(h2, lw['mix_norm'])
    zero_main = jnp.zeros((D_MAIN,), jnp.float32)
    z = mm_bias(u2, wb['w_in_main'], zero_main)[:m].reshape(B_, T, D_MAIN)
    z_g = mm_bias(u2, wb['w_in_gate'], jnp.zeros((LANE,), jnp.float32))[:m, :N_GATE].reshape(B_, T, N_GATE)
    offs = []
    acc = 0
    for n in (2 * D_A, D_B, D_C, KVD, KVD, KVD, KVD, KVD):
        acc += n
        offs.append(acc)
    z_a, z_b, z_q, z_kc, z_vc, z_ks, z_vs, z_kw, z_vw = jnp.split(z, offs, axis=-1)
    if past is None:
        conv_buf = jnp.zeros((B_, CONV_WIDTH - 1, D_A), x.dtype)
        s0_re = jnp.zeros((B_, SSM_GROUPS, SSM_STATE), jnp.float32)
        s0_im = jnp.zeros((B_, SSM_GROUPS, SSM_STATE), jnp.float32)
    else:
        conv_buf, s0_re, s0_im = past['conv'], past['s_re'], past['s_im']
    y_a, conv_new = _conformer_conv(z_a, conv_buf, lw['conv_dw'], lw['conv_dw_bias'], lw['conv_ln_g'], lw['conv_ln_b'])
    y_b, s_re, s_im = _s5_mixer(z_b, s0_re, s0_im, lw, wb)
    y_c, kv_new = _nsa_mixer(z_q, z_kc, z_vc, z_ks, z_vs, z_kw, z_vw, z_g, pos, past, lw)
    zero_d = jnp.zeros((D_MODEL,), jnp.float32)

    def proj(y, w):
        return mm_bias(_pad_rows(y.reshape(m, -1)).astype(jnp.bfloat16), w, zero_d)

    o_a = proj(y_a, wb['conv_w_out'])
    o_b = proj(y_b, wb['ssm_w_out'])
    o_c = proj(y_c, wb['nsa_w_out'])
    gates = mm_bias(u2, wb['merge_w_gate'], lw['merge_b_gate'], act="sigmoid")
    mixed = gates[:, :D_MODEL] * o_a + gates[:, D_MODEL:2 * D_MODEL] * o_b + gates[:, 2 * D_MODEL:] * o_c
    h2 = mm_residual(mixed.astype(jnp.bfloat16), wb['w_out'], h2, 1.0)
    y2 = _ffn(h2, lw['ffn2_norm'], wb['ffn2_w_gate'], wb['ffn2_w_up'], wb['ffn2_w_down'])
    return y2[:m].reshape(B_, T, D_MODEL), (kv_new[0], kv_new[1], kv_new[2], kv_new[3], kv_new[4], kv_new[5], conv_new, s_re, s_im)


def kernel(x_prompt, x_sample, cache_cmp_k, cache_cmp_v, cache_slc_k, cache_slc_v, cache_win_k, cache_win_v, state_conv, state_ssm_re, state_ssm_im, page_table, ffn1_norm, ffn1_w_gate, ffn1_w_up, ffn1_w_down, mix_norm, w_in, conv_dw, conv_dw_bias, conv_ln_g, conv_ln_b, conv_w_out, ssm_lambda_re, ssm_lambda_im, ssm_log_dt, ssm_b_re, ssm_b_im, ssm_c_re, ssm_c_im, ssm_d, ssm_w_glu, ssm_b_glu, ssm_w_out, nsa_q_norm, nsa_k_norm, nsa_cmp_wk, nsa_cmp_wv, nsa_w_out, merge_w_gate, merge_b_gate, w_out, ffn2_norm, ffn2_w_gate, ffn2_w_up, ffn2_w_down):
    past_len = page_table.shape[1] * PAGE_SIZE
    pos_p = jnp.arange(x_prompt.shape[1], dtype=jnp.int32)
    pos_s = past_len + jnp.arange(x_sample.shape[1], dtype=jnp.int32)
    hp, hs = x_prompt, x_sample
    st_p, st_s = [], []
    bf = jnp.bfloat16
    for l in range(DEPTH):
        lw = dict(ffn1_norm=ffn1_norm[l], mix_norm=mix_norm[l],
                  conv_dw=conv_dw[l], conv_dw_bias=conv_dw_bias[l], conv_ln_g=conv_ln_g[l], conv_ln_b=conv_ln_b[l],
                  ssm_lambda_re=ssm_lambda_re[l], ssm_lambda_im=ssm_lambda_im[l], ssm_log_dt=ssm_log_dt[l],
                  ssm_b_re=ssm_b_re[l], ssm_b_im=ssm_b_im[l], ssm_c_re=ssm_c_re[l], ssm_c_im=ssm_c_im[l],
                  ssm_d=ssm_d[l], ssm_b_glu=ssm_b_glu[l],
                  nsa_q_norm=nsa_q_norm[l], nsa_k_norm=nsa_k_norm[l], nsa_cmp_wk=nsa_cmp_wk[l], nsa_cmp_wv=nsa_cmp_wv[l],
                  merge_b_gate=merge_b_gate[l], ffn2_norm=ffn2_norm[l])
        wb = dict(ffn1_w_gate=ffn1_w_gate[l].astype(bf), ffn1_w_up=ffn1_w_up[l].astype(bf), ffn1_w_down=ffn1_w_down[l].astype(bf),
                  w_in_main=w_in[l][:, :D_MAIN].astype(bf),
                  w_in_gate=jnp.pad(w_in[l][:, D_MAIN:], ((0, 0), (0, LANE - N_GATE))).astype(bf),
                  conv_w_out=conv_w_out[l].astype(bf), ssm_w_glu=ssm_w_glu[l].astype(bf), ssm_w_out=ssm_w_out[l].astype(bf),
                  nsa_w_out=nsa_w_out[l].astype(bf), merge_w_gate=merge_w_gate[l].astype(bf), w_out=w_out[l].astype(bf),
                  ffn2_w_gate=ffn2_w_gate[l].astype(bf), ffn2_w_up=ffn2_w_up[l].astype(bf), ffn2_w_down=ffn2_w_down[l].astype(bf))
        hp, new_p = _layer_forward(hp, pos_p, None, lw, wb)
        past = dict(kc=_gather_pages(cache_cmp_k[l], page_table), vc=_gather_pages(cache_cmp_v[l], page_table),
                    ks=_gather_pages(cache_slc_k[l], page_table), vs=_gather_pages(cache_slc_v[l], page_table),
                    win_k=cache_win_k[l], win_v=cache_win_v[l], conv=state_conv[l],
                    s_re=state_ssm_re[l], s_im=state_ssm_im[l])
        hs, new_s = _layer_forward(hs, pos_s, past, lw, wb)
        st_p.append(new_p)
        st_s.append(new_s)
    P = [jnp.stack([st[i] for st in st_p], axis=0) for i in range(9)]
    S = [jnp.stack([st[i] for st in st_s], axis=0) for i in range(9)]
    return (hp, hs, P[0], P[1], P[2], P[3], P[4], P[5], P[6], P[7], P[8], S[0], S[1], S[2], S[3], S[4], S[5], S[6], S[7], S[8])
```

```python
import functools
import math

import jax
import jax.numpy as jnp
from jax import lax
from jax.experimental import pallas as pl
from jax.experimental.pallas import tpu as pltpu

D_MODEL = 4096
DEPTH = 4
PAGE_SIZE = 128
D_A = D_MODEL // 4
D_B = D_MODEL // 4
D_C = D_MODEL // 2
HEAD_DIM = 128
N_HEADS = D_C // HEAD_DIM
KV_HEADS = 4
GQ = N_HEADS // KV_HEADS
KVD = KV_HEADS * HEAD_DIM
CONV_WIDTH = 31
SSM_GROUP = 16
SSM_GROUPS = D_B // SSM_GROUP
SSM_STATE = 64
CMP_BLOCK = 32
CMP_STRIDE = 16
SLC_BLOCK = 64
TOP_K = 16
WINDOW = 512
WIN_QB = 128
SLC_Q_CHUNK = 32
ROPE_DIM = HEAD_DIM // 4
ROPE_THETA = 500000.0
D_FF = ((8 * D_MODEL // 3 + 255) // 256) * 256
EPS = 1e-6
SCALE = HEAD_DIM ** -0.5
NEG = -1e30
FORCE = 1e4
D_MAIN = 2 * D_A + D_B + D_C + 6 * KVD
N_GATE = 3 * N_HEADS
LANE = 128

VMEM_LIMIT_BYTES = 56 * 1024 * 1024


def _cparams(*sem):
    return pltpu.CompilerParams(dimension_semantics=sem, vmem_limit_bytes=VMEM_LIMIT_BYTES)


def _rmsnorm_kernel(x_ref, g_ref, o_ref):
    x = x_ref[...]
    ms = jnp.mean(x * x, axis=-1, keepdims=True)
    o_ref[...] = (x * lax.rsqrt(ms + EPS) * g_ref[...]).astype(o_ref.dtype)


def rmsnorm_rows(x, g, out_dtype=jnp.bfloat16):
    m, d = x.shape
    tr = min(m, 256)
    return pl.pallas_call(
        _rmsnorm_kernel,
        grid=(m // tr,),
        in_specs=[pl.BlockSpec((tr, d), lambda i: (i, 0)), pl.BlockSpec((1, d), lambda i: (0, 0))],
        out_specs=pl.BlockSpec((tr, d), lambda i: (i, 0)),
        out_shape=jax.ShapeDtypeStruct((m, d), out_dtype),
        compiler_params=_cparams("parallel"),
        name="rmsnorm_rows",
    )(x, g.reshape(1, d))


def _swiglu_up_kernel(a_ref, wg_ref, wu_ref, o_ref):
    a = a_ref[...]
    hg = jnp.dot(a, wg_ref[...], preferred_element_type=jnp.float32)
    hu = jnp.dot(a, wu_ref[...], preferred_element_type=jnp.float32)
    o_ref[...] = (hg * jax.nn.sigmoid(hg) * hu).astype(o_ref.dtype)


def swiglu_up(a, wg, wu):
    m, k = a.shape
    f = wg.shape[1]
    tm = min(m, 2048)
    tn = 256
    return pl.pallas_call(
        _swiglu_up_kernel,
        grid=(m // tm, f // tn),
        in_specs=[pl.BlockSpec((tm, k), lambda i, j: (i, 0)),
                  pl.BlockSpec((k, tn), lambda i, j: (0, j)),
                  pl.BlockSpec((k, tn), lambda i, j: (0, j))],
        out_specs=pl.BlockSpec((tm, tn), lambda i, j: (i, j)),
        out_shape=jax.ShapeDtypeStruct((m, f), jnp.bfloat16),
        compiler_params=_cparams("parallel", "arbitrary"),
        name="swiglu_up",
    )(a, wg, wu)


def _mm_res_kernel(a_ref, w_ref, r_ref, o_ref, acc_ref, *, nk, scale):
    kk = pl.program_id(2)
    part = jnp.dot(a_ref[...], w_ref[...], preferred_element_type=jnp.float32)

    @pl.when(kk == 0)
    def _():
        acc_ref[...] = part

    @pl.when(kk > 0)
    def _():
        acc_ref[...] += part

    @pl.when(kk == nk - 1)
    def _():
        o_ref[...] = r_ref[...] + scale * acc_ref[...]


def mm_residual(a, w, res, scale):
    m, k = a.shape
    n = w.shape[1]
    tm = min(m, 1024)
    tn = 512
    tk = k if k <= 4096 else k // 2
    nk = k // tk
    return pl.pallas_call(
        functools.partial(_mm_res_kernel, nk=nk, scale=scale),
        grid=(m // tm, n // tn, nk),
        in_specs=[pl.BlockSpec((tm, tk), lambda i, j, kk: (i, kk)),
                  pl.BlockSpec((tk, tn), lambda i, j, kk: (kk, j)),
                  pl.BlockSpec((tm, tn), lambda i, j, kk: (i, j))],
        out_specs=pl.BlockSpec((tm, tn), lambda i, j, kk: (i, j)),
        out_shape=jax.ShapeDtypeStruct((m, n), jnp.float32),
        scratch_shapes=[pltpu.VMEM((tm, tn), jnp.float32)],
        compiler_params=_cparams("parallel", "arbitrary", "arbitrary"),
        name="mm_residual",
    )(a, w, res)


def _mm_bias_kernel(a_ref, w_ref, b_ref, o_ref, *, act):
    y = jnp.dot(a_ref[...], w_ref[...], preferred_element_type=jnp.float32) + b_ref[...]
    if act == "sigmoid":
        y = jax.nn.sigmoid(y)
    o_ref[...] = y.astype(o_ref.dtype)


def mm_bias(a, w, bias, act=None, out_dtype=jnp.float32):
    m, k = a.shape
    n = w.shape[1]
    tm = min(m, 1024)
    tn = 512 if n % 512 == 0 else n
    return pl.pallas_call(
        functools.partial(_mm_bias_kernel, act=act),
        grid=(m // tm, n // tn),
        in_specs=[pl.BlockSpec((tm, k), lambda i, j: (i, 0)),
                  pl.BlockSpec((k, tn), lambda i, j: (0, j)),
                  pl.BlockSpec((1, tn), lambda i, j: (0, j))],
        out_specs=pl.BlockSpec((tm, tn), lambda i, j: (i, j)),
        out_shape=jax.ShapeDtypeStruct((m, n), out_dtype),
        compiler_params=_cparams("parallel", "arbitrary"),
        name="mm_bias",
    )(a, w, bias.reshape(1, n))


N_STATE = SSM_GROUPS * SSM_STATE
S5_LANE_GROUPS = D_B // LANE
S5_SLAB = N_STATE // S5_LANE_GROUPS
S5_ROWS = 256


def _s5_kernel(z_ref, s0_ref, lr_ref, li_ref, wre_ref, wim_ref, cr_ref, ci_ref, d_ref, wglu_ref, bglu_ref,
               y_ref, slast_ref, bre_ref, bim_ref, sre_ref, sim_ref, yacc_ref):
    tt = z_ref.shape[0]

    @pl.when(pl.program_id(1) == 0)
    def _():
        sre_ref[...] = s0_ref[:, :N_STATE]
        sim_ref[...] = s0_ref[:, N_STATE:]

    x = z_ref[...]
    xb = x.astype(jnp.bfloat16)
    for j in range(S5_LANE_GROUPS):
        xj = xb[:, j * LANE:(j + 1) * LANE]
        cs = slice(j * S5_SLAB, (j + 1) * S5_SLAB)
        bre_ref[:, cs] = jnp.dot(xj, wre_ref[j], preferred_element_type=jnp.float32)
        bim_ref[:, cs] = jnp.dot(xj, wim_ref[j], preferred_element_type=jnp.float32)

    for c in range(S5_LANE_GROUPS):
        cs = slice(c * S5_SLAB, (c + 1) * S5_SLAB)
        lr = lr_ref[:, cs]
        li = li_ref[:, cs]

        def step(t, carry, cs=cs, lr=lr, li=li):
            sr, si = carry
            nr = lr * sr - li * si + bre_ref[pl.ds(t, 1), cs]
            ni = lr * si + li * sr + bim_ref[pl.ds(t, 1), cs]
            bre_ref[pl.ds(t, 1), cs] = nr
            bim_ref[pl.ds(t, 1), cs] = ni
            return nr, ni

        sr, si = lax.fori_loop(0, tt, step, (sre_ref[:, cs], sim_ref[:, cs]), unroll=8)
        sre_ref[:, cs] = sr
        sim_ref[:, cs] = si

    for j in range(S5_LANE_GROUPS):
        cs = slice(j * S5_SLAB, (j + 1) * S5_SLAB)
        yj = jnp.dot(bre_ref[:, cs].astype(jnp.bfloat16), cr_ref[j], preferred_element_type=jnp.float32)
        yj = yj + jnp.dot(bim_ref[:, cs].astype(jnp.bfloat16), ci_ref[j], preferred_element_type=jnp.float32)
        yacc_ref[:, j * LANE:(j + 1) * LANE] = yj
    y = jax.nn.gelu(yacc_ref[...] + d_ref[...] * x)
    gl = jnp.dot(y.astype(jnp.bfloat16), wglu_ref[...], preferred_element_type=jnp.float32) + bglu_ref[...]
    y_ref[...] = (y * jax.nn.sigmoid(gl)).astype(y_ref.dtype)
    slast_ref[:, :N_STATE] = sre_ref[...]
    slast_ref[:, N_STATE:] = sim_ref[...]


def _s5_params(lw):
    lam = lax.complex(lw['ssm_lambda_re'], lw['ssm_lambda_im'])
    dt = jnp.exp(lw['ssm_log_dt'])[:, None]
    lam_bar = jnp.exp(lam * dt)
    b_bar = ((lam_bar - 1.0) / lam)[..., None] * lax.complex(lw['ssm_b_re'], lw['ssm_b_im'])
    gpl = LANE // SSM_GROUP
    eye = jnp.eye(gpl, dtype=jnp.float32)

    def bdiag_in(b):
        b4 = b.reshape(S5_LANE_GROUPS, gpl, SSM_STATE, SSM_GROUP)
        m = jnp.einsum('jgpc,gh->jgchp', b4, eye)
        return m.reshape(S5_LANE_GROUPS, LANE, S5_SLAB).astype(jnp.bfloat16)

    def bdiag_out(cm):
        c4 = cm.reshape(S5_LANE_GROUPS, gpl, SSM_GROUP, SSM_STATE)
        m = jnp.einsum('jgcp,gh->jgphc', c4, eye)
        return m.reshape(S5_LANE_GROUPS, S5_SLAB, LANE).astype(jnp.bfloat16)

    return dict(lr=jnp.real(lam_bar).reshape(1, N_STATE), li=jnp.imag(lam_bar).reshape(1, N_STATE),
                wre=bdiag_in(jnp.real(b_bar)), wim=bdiag_in(jnp.imag(b_bar)),
                cr=bdiag_out(lw['ssm_c_re']), ci=bdiag_out(-lw['ssm_c_im']))


def s5_prompt(z3, s0, sp, d, wglu, bglu):
    b, t, _ = z3.shape
    tt = min(t, S5_ROWS)
    zcol = (2 * D_A) // D_B
    const3 = lambda bi, ti: (0, 0, 0)
    const2 = lambda bi, ti: (0, 0)
    return pl.pallas_call(
        _s5_kernel,
        grid=(b, t // tt),
        in_specs=[pl.BlockSpec((None, tt, D_B), lambda bi, ti: (bi, ti, zcol)),
                  pl.BlockSpec((None, 1, 2 * N_STATE), lambda bi, ti: (bi, 0, 0)),
                  pl.BlockSpec((1, N_STATE), const2), pl.BlockSpec((1, N_STATE), const2),
                  pl.BlockSpec((S5_LANE_GROUPS, LANE, S5_SLAB), const3), pl.BlockSpec((S5_LANE_GROUPS, LANE, S5_SLAB), const3),
                  pl.BlockSpec((S5_LANE_GROUPS, S5_SLAB, LANE), const3), pl.BlockSpec((S5_LANE_GROUPS, S5_SLAB, LANE), const3),
                  pl.BlockSpec((1, D_B), const2), pl.BlockSpec((D_B, D_B), const2), pl.BlockSpec((1, D_B), const2)],
        out_specs=[pl.BlockSpec((None, tt, D_B), lambda bi, ti: (bi, ti, 0)),
                   pl.BlockSpec((None, 1, 2 * N_STATE), lambda bi, ti: (bi, 0, 0))],
        out_shape=[jax.ShapeDtypeStruct((b, t, D_B), jnp.bfloat16),
                   jax.ShapeDtypeStruct((b, 1, 2 * N_STATE), jnp.float32)],
        scratch_shapes=[pltpu.VMEM((tt, N_STATE), jnp.float32), pltpu.VMEM((tt, N_STATE), jnp.float32),
                        pltpu.VMEM((1, N_STATE), jnp.float32), pltpu.VMEM((1, N_STATE), jnp.float32),
                        pltpu.VMEM((tt, D_B), jnp.float32)],
        compiler_params=_cparams("parallel", "arbitrary"),
        name="s5_prompt",
    )(z3, s0, sp['lr'], sp['li'], sp['wre'], sp['wim'], sp['cr'], sp['ci'], d.reshape(1, D_B), wglu, bglu.reshape(1, D_B))


CONV_HIST = 32
CONV_ROWS = 256


def _conv_kernel(zl_ref, zg_ref, buf_ref, dw_ref, dwb_ref, lng_ref, lnb_ref, y_ref, tail_ref, xpad_ref):
    tt = zl_ref.shape[0]

    @pl.when(pl.program_id(1) == 0)
    def _():
        xpad_ref[0:CONV_HIST, :] = buf_ref[...]

    xpad_ref[CONV_HIST:CONV_HIST + tt, :] = zl_ref[...] * jax.nn.sigmoid(zg_ref[...])
    off = CONV_HIST - (CONV_WIDTH - 1)
    acc = xpad_ref[off:off + tt, :] * dw_ref[0:1, :]
    for k in range(1, CONV_WIDTH):
        acc = acc + xpad_ref[off + k:off + k + tt, :] * dw_ref[k:k + 1, :]
    y = acc + dwb_ref[...]
    mu = jnp.mean(y, axis=-1, keepdims=True)
    yc = y - mu
    var = jnp.mean(yc * yc, axis=-1, keepdims=True)
    yn = yc * lax.rsqrt(var + EPS) * lng_ref[...] + lnb_ref[...]
    y_ref[...] = (yn * jax.nn.sigmoid(yn)).astype(y_ref.dtype)
    tail = xpad_ref[tt:tt + CONV_HIST, :]
    tail_ref[...] = tail
    xpad_ref[0:CONV_HIST, :] = tail


def conv_prompt(z3, buf, dw, dwb, lng, lnb):
    b, t, _ = z3.shape
    tt = min(t, CONV_ROWS)
    const2 = lambda bi, ti: (0, 0)
    return pl.pallas_call(
        _conv_kernel,
        grid=(b, t // tt),
        in_specs=[pl.BlockSpec((None, tt, D_A), lambda bi, ti: (bi, ti, 0)),
                  pl.BlockSpec((None, tt, D_A), lambda bi, ti: (bi, ti, 1)),
                  pl.BlockSpec((None, CONV_HIST, D_A), lambda bi, ti: (bi, 0, 0)),
                  pl.BlockSpec((CONV_HIST, D_A), const2), pl.BlockSpec((1, D_A), const2),
                  pl.BlockSpec((1, D_A), const2), pl.BlockSpec((1, D_A), const2)],
        out_specs=[pl.BlockSpec((None, tt, D_A), lambda bi, ti: (bi, ti, 0)),
                   pl.BlockSpec((None, CONV_HIST, D_A), lambda bi, ti: (bi, 0, 0))],
        out_shape=[jax.ShapeDtypeStruct((b, t, D_A), jnp.bfloat16),
                   jax.ShapeDtypeStruct((b, CONV_HIST, D_A), jnp.float32)],
        scratch_shapes=[pltpu.VMEM((tt + CONV_HIST, D_A), jnp.float32)],
        compiler_params=_cparams("parallel", "arbitrary"),
        name="conv_prompt",
    )(z3, z3, buf, jnp.pad(dw, ((0, CONV_HIST - CONV_WIDTH), (0, 0))), dwb.reshape(1, D_A), lng.reshape(1, D_A), lnb.reshape(1, D_A))


NSA_ROWS = 256
Q_COL0 = 2 * D_A + D_B
KV_COL0 = Q_COL0 + D_C
CMP_PER_ROW = 16 * KVD


def _head_rmsnorm(x, g):
    ms = jnp.mean(x * x, axis=-1, keepdims=True)
    return x * lax.rsqrt(ms + EPS) * g


def _rope(x, cosf, sinf, lane):
    half = ROPE_DIM // 2
    rot = jnp.where(lane < half, pltpu.roll(x, HEAD_DIM - half, 1), pltpu.roll(x, half, 1))
    return x * cosf + rot * sinf


def _nsa_prep_kernel(zq0_ref, zq1_ref, kc_ref, vc_ref, ks_ref, vs_ref, kw_ref, vw_ref, cos_ref, sin_ref, qg_ref, kg_ref,
                     qn_ref, qr_ref, okc_ref, ovc_ref, oks_ref, ovs_ref, okw_ref, ovw_ref,
                     bks_ref, bvs_ref, bkw_ref, bvw_ref):
    tt = cos_ref.shape[0]
    cosf = cos_ref[...]
    sinf = sin_ref[...]
    lane = lax.broadcasted_iota(jnp.int32, (tt, HEAD_DIM), 1)
    qg = qg_ref[...]
    heads_per_ref = N_HEADS // 2
    for h in range(N_HEADS):
        src = zq0_ref if h < heads_per_ref else zq1_ref
        hh = h % heads_per_ref
        xn = _head_rmsnorm(src[:, hh * HEAD_DIM:(hh + 1) * HEAD_DIM], qg)
        qn_ref[:, h * HEAD_DIM:(h + 1) * HEAD_DIM] = xn.astype(qn_ref.dtype)
        qr_ref[:, h * HEAD_DIM:(h + 1) * HEAD_DIM] = _rope(xn, cosf, sinf, lane).astype(qr_ref.dtype)
    okc_ref[...] = kc_ref[...]
    ovc_ref[...] = vc_ref[...]
    vs = vs_ref[...]
    ovs_ref[...] = vs
    bvs_ref[...] = vs.astype(bvs_ref.dtype)
    vw = vw_ref[...]
    ovw_ref[...] = vw
    bvw_ref[...] = vw.astype(bvw_ref.dtype)
    for h in range(KV_HEADS):
        hs = slice(h * HEAD_DIM, (h + 1) * HEAD_DIM)
        ks = _rope(_head_rmsnorm(ks_ref[:, hs], kg_ref[1:2, :]), cosf, sinf, lane)
        oks_ref[:, hs] = ks
        bks_ref[:, hs] = ks.astype(bks_ref.dtype)
        kw = _rope(_head_rmsnorm(kw_ref[:, hs], kg_ref[2:3, :]), cosf, sinf, lane)
        okw_ref[:, hs] = kw
        bkw_ref[:, hs] = kw.astype(bkw_ref.dtype)


def nsa_prep(z3, cosf, sinf, q_norm, k_norm):
    b, t, _ = z3.shape
    tt = min(t, NSA_ROWS)
    half_q = D_C // 2
    kvb = KV_COL0 // KVD

    def zcol(width, idx):
        return pl.BlockSpec((None, tt, width), lambda bi, ti: (bi, ti, idx))

    row = lambda width: pl.BlockSpec((None, tt, width), lambda bi, ti: (bi, ti, 0))
    const2 = lambda bi, ti: (0, 0)
    f32, bf16 = jnp.float32, jnp.bfloat16
    kv_f32 = jax.ShapeDtypeStruct((b, t, KVD), f32)
    kv_bf = jax.ShapeDtypeStruct((b, t, KVD), bf16)
    return pl.pallas_call(
        _nsa_prep_kernel,
        grid=(b, t // tt),
        in_specs=[zcol(half_q, Q_COL0 // half_q), zcol(half_q, Q_COL0 // half_q + 1)]
                 + [zcol(KVD, kvb + i) for i in range(6)]
                 + [pl.BlockSpec((tt, HEAD_DIM), lambda bi, ti: (ti, 0)), pl.BlockSpec((tt, HEAD_DIM), lambda bi, ti: (ti, 0)),
                    pl.BlockSpec((1, HEAD_DIM), const2), pl.BlockSpec((3, HEAD_DIM), const2)],
        out_specs=[row(D_C), row(D_C)] + [row(KVD)] * 10,
        out_shape=[jax.ShapeDtypeStruct((b, t, D_C), bf16), jax.ShapeDtypeStruct((b, t, D_C), bf16)]
                  + [kv_f32] * 6 + [kv_bf] * 4,
        compiler_params=_cparams("parallel", "arbitrary"),
        name="nsa_prep",
    )(z3, z3, z3, z3, z3, z3, z3, z3, cosf, sinf, q_norm.reshape(1, HEAD_DIM), k_norm)


def _compress_kernel(xk_ref, xks_ref, xv_ref, xvs_ref, wk_ref, wv_ref, kg_ref, kc_ref, vc_ref):
    half = CMP_BLOCK // 2

    def pooled(x_ref, xs_ref, w_ref):
        acc = x_ref[:, 0:KVD] * w_ref[0:1, :] + xs_ref[:, 0:KVD] * w_ref[half:half + 1, :]
        for i in range(1, half):
            cs = slice(i * KVD, (i + 1) * KVD)
            acc = acc + x_ref[:, cs] * w_ref[i:i + 1, :] + xs_ref[:, cs] * w_ref[half + i:half + i + 1, :]
        return acc

    kc = pooled(xk_ref, xks_ref, wk_ref)
    vc = pooled(xv_ref, xvs_ref, wv_ref)
    vc_ref[...] = vc.astype(vc_ref.dtype)
    for h in range(KV_HEADS):
        hs = slice(h * HEAD_DIM, (h + 1) * HEAD_DIM)
        kc_ref[:, hs] = _head_rmsnorm(kc[:, hs], kg_ref[0:1, :]).astype(kc_ref.dtype)


def nsa_compress(kc_rows, vc_rows, cmp_wk, cmp_wv, k_norm):
    b, t, _ = kc_rows.shape
    nb = t // CMP_STRIDE
    rb = min(nb, 64)

    def views(x):
        x2 = x.reshape(b, nb, CMP_PER_ROW)
        return x2, jnp.concatenate([x2[:, 1:], jnp.zeros((b, 1, CMP_PER_ROW), x.dtype)], axis=1)

    xk, xks = views(kc_rows)
    xv, xvs = views(vc_rows)
    blk = pl.BlockSpec((None, rb, CMP_PER_ROW), lambda bi, ri: (bi, ri, 0))
    const2 = lambda bi, ri: (0, 0)
    out = pl.BlockSpec((None, rb, KVD), lambda bi, ri: (bi, ri, 0))
    return pl.pallas_call(
        _compress_kernel,
        grid=(b, nb // rb),
        in_specs=[blk, blk, blk, blk, pl.BlockSpec((CMP_BLOCK, KVD), const2), pl.BlockSpec((CMP_BLOCK, KVD), const2),
                  pl.BlockSpec((3, HEAD_DIM), const2)],
        out_specs=[out, out],
        out_shape=[jax.ShapeDtypeStruct((b, nb, KVD), jnp.bfloat16)] * 2,
        compiler_params=_cparams("parallel", "arbitrary"),
        name="nsa_compress",
    )(xk, xks, xv, xvs, cmp_wk.transpose(1, 0, 2).reshape(CMP_BLOCK, KVD), cmp_wv.transpose(1, 0, 2).reshape(CMP_BLOCK, KVD), k_norm)


def _dot_nt(a, b):
    return lax.dot_general(a, b, (((1,), (1,)), ((), ())), preferred_element_type=jnp.float32)


def _softmax_rows(s, mask):
    s = jnp.where(mask, s, NEG)
    m = jnp.max(s, axis=-1, keepdims=True)
    e = jnp.where(mask, jnp.exp(s - m), 0.0)
    return e / jnp.maximum(jnp.sum(e, axis=-1, keepdims=True), 1e-30)


def _nsa_attn_kernel(qn_ref, qr_ref, kc_ref, vc_ref, ks_ref, vs_ref, kw_ref, vw_ref, zg_ref, ov_ref, ex_ref, o_ref):
    tq = qn_ref.shape[0]
    t_all = ks_ref.shape[0]
    n_cmp_pad = kc_ref.shape[0]
    t0 = pl.program_id(2) * tq
    f32, bf16 = jnp.float32, jnp.bfloat16

    qn = jnp.concatenate([qn_ref[:, g * HEAD_DIM:(g + 1) * HEAD_DIM] for g in range(GQ)], axis=0)
    s = _dot_nt(qn, kc_ref[...]) * SCALE
    rows = lax.broadcasted_iota(jnp.int32, (GQ * tq, n_cmp_pad), 0)
    tpos_c = t0 + (rows & (tq - 1))
    jblk = lax.broadcasted_iota(jnp.int32, (GQ * tq, n_cmp_pad), 1)
    p = _softmax_rows(s, jblk * CMP_STRIDE + (CMP_BLOCK - 1) <= tpos_c)
    o_cmp = jnp.dot(p.astype(bf16), vc_ref[...], preferred_element_type=f32)
    psum = p[0:tq]
    for g in range(1, GQ):
        psum = psum + p[g * tq:(g + 1) * tq]

    imp = jnp.dot(psum, ov_ref[...], preferred_element_type=f32, precision=lax.Precision.HIGHEST)
    n_lanes = ov_ref.shape[1]
    n_slc = t_all // SLC_BLOCK
    nb = lax.broadcasted_iota(jnp.int32, (tq, n_lanes), 1)
    tpos = t0 + lax.broadcasted_iota(jnp.int32, (tq, n_lanes), 0)
    cur = tpos // SLC_BLOCK
    forced = (nb == 0) | (nb == cur) | (nb == cur - 1)
    valid = nb <= cur
    score = jnp.where(valid, jnp.where(forced, FORCE, imp), -FORCE)
    rank = jnp.zeros((tq, n_lanes), f32)
    for m in range(n_slc):
        col = score[:, m:m + 1]
        ahead = (col > score) | ((col == score) & (nb > m))
        rank = rank + jnp.where(ahead, 1.0, 0.0)
    sel = jnp.where((rank < float(min(TOP_K, n_slc))) & valid, 1.0, 0.0)
    allowed = jnp.dot(sel.astype(bf16), ex_ref[...], preferred_element_type=f32)
    kpos = lax.broadcasted_iota(jnp.int32, (tq, t_all), 1)
    tq_pos = t0 + lax.broadcasted_iota(jnp.int32, (tq, t_all), 0)
    mask_s = (allowed > 0.5) & (kpos <= tq_pos)

    span = min(WINDOW + tq, t_all)
    w0 = pl.multiple_of(jnp.maximum(t0 + tq - span, 0), tq)
    kwin = kw_ref[pl.ds(w0, span), :]
    vwin = vw_ref[pl.ds(w0, span), :]
    wpos = w0 + lax.broadcasted_iota(jnp.int32, (tq, span), 1)
    wdiff = t0 + lax.broadcasted_iota(jnp.int32, (tq, span), 0) - wpos
    mask_w = (wdiff >= 0) & (wdiff <= WINDOW)

    gate = jax.nn.sigmoid(zg_ref[...])
    ks = ks_ref[...]
    vs = vs_ref[...]
    for g in range(GQ):
        hs = slice(g * HEAD_DIM, (g + 1) * HEAD_DIM)
        qr = qr_ref[:, hs]
        ps = _softmax_rows(_dot_nt(qr, ks) * SCALE, mask_s)
        o_slc = jnp.dot(ps.astype(bf16), vs, preferred_element_type=f32)
        pw = _softmax_rows(_dot_nt(qr, kwin) * SCALE, mask_w)
        o_win = jnp.dot(pw.astype(bf16), vwin, preferred_element_type=f32)
        o = (gate[:, 3 * g:3 * g + 1] * o_cmp[g * tq:(g + 1) * tq]
             + gate[:, 3 * g + 1:3 * g + 2] * o_slc + gate[:, 3 * g + 2:3 * g + 3] * o_win)
        o_ref[:, hs] = o.astype(o_ref.dtype)


def nsa_attention(qn, qr, kcmp, vcmp, ks, vs, kw, vw, zg):
    b, t, _ = qn.shape
    tq = min(t, WIN_QB)
    n_cmp_pad = kcmp.shape[1]
    n_slc = t // SLC_BLOCK
    starts = jnp.arange(n_cmp_pad, dtype=jnp.int32) * CMP_STRIDE
    blk_start = jnp.arange(LANE, dtype=jnp.int32) * SLC_BLOCK
    overlap = ((starts[:, None] < blk_start[None, :] + SLC_BLOCK) & (starts[:, None] + CMP_BLOCK > blk_start[None, :])
               & (jnp.arange(LANE)[None, :] < n_slc) & (starts[:, None] + CMP_BLOCK <= t)).astype(jnp.float32)
    expand = (jnp.arange(LANE, dtype=jnp.int32)[:, None] == (jnp.arange(t, dtype=jnp.int32) // SLC_BLOCK)[None, :]).astype(jnp.bfloat16)
    qspec = pl.BlockSpec((None, tq, GQ * HEAD_DIM), lambda bi, hi, qi: (bi, qi, hi))
    kvspec = lambda rows: pl.BlockSpec((None, rows, HEAD_DIM), lambda bi, hi, qi: (bi, 0, hi))
    const2 = lambda bi, hi, qi: (0, 0)
    return pl.pallas_call(
        _nsa_attn_kernel,
        grid=(b, KV_HEADS, t // tq),
        in_specs=[qspec, qspec, kvspec(n_cmp_pad), kvspec(n_cmp_pad), kvspec(t), kvspec(t), kvspec(t), kvspec(t),
                  pl.BlockSpec((None, tq, LANE), lambda bi, hi, qi: (bi, qi, hi)),
                  pl.BlockSpec((n_cmp_pad, LANE), const2), pl.BlockSpec((LANE, t), const2)],
        out_specs=qspec,
        out_shape=jax.ShapeDtypeStruct((b, t, D_C), jnp.bfloat16),
        compiler_params=_cparams("parallel", "parallel", "arbitrary"),
        name="nsa_attention",
    )(qn, qr, kcmp, vcmp, ks, vs, kw, vw, zg, overlap, expand)


def _rope_tables(pos):
    half = ROPE_DIM // 2
    inv_freq = ROPE_THETA ** (-jnp.arange(half, dtype=jnp.float32) / half)
    ang = pos.astype(jnp.float32)[:, None] * inv_freq[None, :]
    cos, sin = jnp.cos(ang), jnp.sin(ang)
    t = pos.shape[0]
    cosf = jnp.concatenate([cos, cos, jnp.ones((t, HEAD_DIM - ROPE_DIM), jnp.float32)], axis=1)
    sinf = jnp.concatenate([-sin, sin, jnp.zeros((t, HEAD_DIM - ROPE_DIM), jnp.float32)], axis=1)
    return cosf, sinf


def _rmsnorm(x, g):
    xf = x.astype(jnp.float32)
    y = xf * lax.rsqrt(jnp.mean(xf * xf, axis=-1, keepdims=True) + EPS)
    return (y * g.astype(jnp.float32)).astype(x.dtype)


def _layernorm(x, g, b):
    xf = x.astype(jnp.float32)
    mu = jnp.mean(xf, axis=-1, keepdims=True)
    var = jnp.mean(jnp.square(xf - mu), axis=-1, keepdims=True)
    y = (xf - mu) * lax.rsqrt(var + EPS)
    return (y * g.astype(jnp.float32) + b.astype(jnp.float32)).astype(x.dtype)


def _partial_rope(x, pos):
    half = ROPE_DIM // 2
    inv_freq = ROPE_THETA ** (-jnp.arange(half, dtype=jnp.float32) / half)
    ang = pos.astype(jnp.float32)[:, None] * inv_freq[None, :]
    cos = jnp.cos(ang)[None, :, None, :]
    sin = jnp.sin(ang)[None, :, None, :]
    xr = x[..., :ROPE_DIM].astype(jnp.float32)
    x1, x2 = xr[..., :half], xr[..., half:]
    rot = jnp.concatenate([x1 * cos - x2 * sin, x2 * cos + x1 * sin], axis=-1)
    return jnp.concatenate([rot.astype(x.dtype), x[..., ROPE_DIM:]], axis=-1)


def _masked_softmax(s, mask):
    s = jnp.where(mask, s, NEG)
    m = jnp.max(s, axis=-1, keepdims=True)
    e = jnp.where(mask, jnp.exp(s - m), 0.0)
    return e / jnp.maximum(jnp.sum(e, axis=-1, keepdims=True), 1e-30)


def _gqa_attend(q, k, v, mask):
    s = jnp.einsum('...qkgd,...skd->...qkgs', q, k).astype(jnp.float32) * SCALE
    p = _masked_softmax(s, mask)
    return jnp.einsum('...qkgs,...skd->...qkgd', p.astype(v.dtype), v), p


def _gather_pages(pool, page_table):
    g = pool[page_table]
    return g.reshape(g.shape[0], g.shape[1] * g.shape[2], *g.shape[3:])


def _causal_depthwise_conv(xpad, w):
    return lax.conv_general_dilated(xpad, w[:, None, :].astype(xpad.dtype), window_strides=(1,), padding='VALID',
                                    dimension_numbers=('NWC', 'WIO', 'NWC'), feature_group_count=xpad.shape[-1])


def _conformer_conv(z_a, buf, dw, dw_b, ln_g, ln_b):
    a = z_a[..., :D_A] * jax.nn.sigmoid(z_a[..., D_A:])
    xpad = jnp.concatenate([buf.astype(a.dtype), a], axis=1)
    y = _causal_depthwise_conv(xpad, dw) + dw_b
    y = jax.nn.silu(_layernorm(y, ln_g, ln_b))
    return y, xpad[:, -(CONV_WIDTH - 1):]


def _scan_combine(e1, e2):
    a1, b1 = e1
    a2, b2 = e2
    return a2 * a1, a2 * b1 + b2


def _s5_mixer(xb, s0_re, s0_im, lw, wb):
    B_, T, _ = xb.shape
    xf = xb.astype(jnp.float32)
    u = xf.reshape(B_, T, SSM_GROUPS, SSM_GROUP).astype(jnp.complex64)
    lam = lax.complex(lw['ssm_lambda_re'], lw['ssm_lambda_im'])
    dt = jnp.exp(lw['ssm_log_dt'])[:, None]
    lam_bar = jnp.exp(lam * dt)
    bmat = lax.complex(lw['ssm_b_re'], lw['ssm_b_im'])
    b_bar = ((lam_bar - 1.0) / lam)[..., None] * bmat
    bu = jnp.einsum('btgc,gpc->btgp', u, b_bar)
    a = jnp.broadcast_to(lam_bar, bu.shape)
    a_cum, s = lax.associative_scan(_scan_combine, (a, bu), axis=1)
    s0 = lax.complex(s0_re, s0_im)
    s = s + a_cum * s0[:, None]
    cmat = lax.complex(lw['ssm_c_re'], lw['ssm_c_im'])
    y = jnp.real(jnp.einsum('btgp,gcp->btgc', s, cmat)).reshape(B_, T, D_B) + lw['ssm_d'] * xf
    y = jax.nn.gelu(y)
    yg = mm_bias(_pad_rows(y.reshape(B_ * T, D_B)).astype(jnp.bfloat16), wb['ssm_w_glu'], lw['ssm_b_glu'], act="sigmoid")
    y = y * yg[:B_ * T].reshape(B_, T, D_B)
    s_last = s[:, -1]
    return y, jnp.real(s_last), jnp.imag(s_last)


def _select_attend(q, sel, ok, pos, kb, vb):
    bi = jnp.arange(q.shape[0])[:, None, None, None]
    hi = jnp.arange(KV_HEADS)[None, None, :, None]
    kg = kb[bi, hi, sel]
    vg = vb[bi, hi, sel]
    s = jnp.einsum('btkgd,btknsd->btkgns', q, kg).astype(jnp.float32) * SCALE
    tok = sel[..., None] * SLC_BLOCK + jnp.arange(SLC_BLOCK, dtype=sel.dtype)
    mask = (tok <= pos[None, :, None, None, None]) & ok[..., None]
    shp = s.shape
    p = _masked_softmax(s.reshape(*shp[:4], -1), mask.reshape(*mask.shape[:3], 1, -1)).reshape(shp)
    return jnp.einsum('btkgns,btknsd->btkgd', p.astype(vg.dtype), vg)


def _nsa_compress_select(qn, qr, kc_rows, vc_rows, ks_rows, vs_rows, pos, cmp_wk, cmp_wv, k_norm_cmp):
    B_, L = kc_rows.shape[:2]
    T = qn.shape[1]
    n_cmp = (L - CMP_BLOCK) // CMP_STRIDE + 1
    starts = jnp.arange(n_cmp, dtype=jnp.int32) * CMP_STRIDE
    idx = starts[:, None] + jnp.arange(CMP_BLOCK, dtype=jnp.int32)[None, :]
    k_cmp = jnp.einsum('bjikd,kid->bjkd', kc_rows[:, idx], cmp_wk)
    v_cmp = jnp.einsum('bjikd,kid->bjkd', vc_rows[:, idx], cmp_wv)
    k_cmp = _rmsnorm(k_cmp, k_norm_cmp)
    mask_c = (starts + CMP_BLOCK - 1)[None, :] <= pos[:, None]
    o_cmp, p_cmp = _gqa_attend(qn, k_cmp, v_cmp, mask_c[None, :, None, None, :])
    n_slc = -(-L // SLC_BLOCK)
    blk = jnp.arange(n_slc, dtype=jnp.int32)
    blk_start = blk * SLC_BLOCK
    overlap = ((starts[:, None] < blk_start[None, :] + SLC_BLOCK) & (starts[:, None] + CMP_BLOCK > blk_start[None, :])).astype(jnp.float32)
    imp = jnp.einsum('btkgj,jn->btkn', p_cmp, overlap)
    cur = pos // SLC_BLOCK
    forced = (blk[None, :] == 0) | (blk[None, :] == cur[:, None]) | (blk[None, :] == cur[:, None] - 1)
    valid = blk[None, :] <= cur[:, None]
    score = jnp.where(forced[None, :, None, :], FORCE, imp)
    score = jnp.where(valid[None, :, None, :], score, -FORCE)
    _, sel = lax.top_k(score, min(TOP_K, n_slc))
    sel_ok = sel <= cur[None, :, None, None]
    pad = n_slc * SLC_BLOCK - L
    ks_b = jnp.pad(ks_rows, ((0, 0), (0, pad), (0, 0), (0, 0))).reshape(B_, n_slc, SLC_BLOCK, KV_HEADS, HEAD_DIM).transpose(0, 3, 1, 2, 4)
    vs_b = jnp.pad(vs_rows, ((0, 0), (0, pad), (0, 0), (0, 0))).reshape(B_, n_slc, SLC_BLOCK, KV_HEADS, HEAD_DIM).transpose(0, 3, 1, 2, 4)
    chunk = SLC_Q_CHUNK if T % SLC_Q_CHUNK == 0 else T
    n_chunks = T // chunk

    def to_chunks(a):
        return a.reshape(a.shape[0], n_chunks, chunk, *a.shape[2:]).swapaxes(0, 1)

    def attend_chunk(args):
        q_c, sel_c, ok_c, pos_c = args
        return _select_attend(q_c, sel_c, ok_c, pos_c, ks_b, vs_b)

    o = lax.map(attend_chunk, (to_chunks(qr), to_chunks(sel), to_chunks(sel_ok), pos.reshape(n_chunks, chunk)))
    o_slc = o.swapaxes(0, 1).reshape(qr.shape)
    return o_cmp, o_slc


def _band_window(qr, kw, vw):
    B_, T = qr.shape[:2]
    nb = T // WIN_QB
    span = WINDOW + WIN_QB
    kpad = jnp.pad(kw, ((0, 0), (WINDOW, 0), (0, 0), (0, 0)))
    vpad = jnp.pad(vw, ((0, 0), (WINDOW, 0), (0, 0), (0, 0)))
    idx = jnp.arange(nb, dtype=jnp.int32)[:, None] * WIN_QB + jnp.arange(span, dtype=jnp.int32)[None, :]
    qpos = jnp.arange(nb, dtype=jnp.int32)[:, None] * WIN_QB + jnp.arange(WIN_QB, dtype=jnp.int32)[None, :]
    kpos = idx - WINDOW
    diff = qpos[:, :, None] - kpos[:, None, :]
    mask = (diff >= 0) & (diff <= WINDOW) & (kpos[:, None, :] >= 0)
    o, _ = _gqa_attend(qr.reshape(B_, nb, WIN_QB, KV_HEADS, GQ, HEAD_DIM), kpad[:, idx], vpad[:, idx], mask[None, :, :, None, None, :])
    return o.reshape(qr.shape)


def _nsa_mixer(z_q, z_kc, z_vc, z_ks, z_vs, z_kw, z_vw, z_g, pos, past, lw):
    B_, T = z_q.shape[:2]
    q = z_q.reshape(B_, T, N_HEADS, HEAD_DIM)
    qn = _rmsnorm(q, lw['nsa_q_norm'])
    qr = _partial_rope(qn, pos).reshape(B_, T, KV_HEADS, GQ, HEAD_DIM)
    qn = qn.reshape(B_, T, KV_HEADS, GQ, HEAD_DIM)
    kvshape = (B_, T, KV_HEADS, HEAD_DIM)
    kc_new = z_kc.reshape(kvshape)
    vc_new = z_vc.reshape(kvshape)
    ks_new = _partial_rope(_rmsnorm(z_ks.reshape(kvshape), lw['nsa_k_norm'][1]), pos)
    vs_new = z_vs.reshape(kvshape)
    kw_new = _partial_rope(_rmsnorm(z_kw.reshape(kvshape), lw['nsa_k_norm'][2]), pos)
    vw_new = z_vw.reshape(kvshape)
    if past is None:
        kc_rows, vc_rows, ks_rows, vs_rows = kc_new, vc_new, ks_new, vs_new
        o_win = _band_window(qr, kw_new, vw_new)
        keep = min(WINDOW, T)
        win_k_new = kw_new[:, -keep:]
        win_v_new = vw_new[:, -keep:]
    else:
        kc_rows = jnp.concatenate([past['kc'], kc_new], axis=1)
        vc_rows = jnp.concatenate([past['vc'], vc_new], axis=1)
        ks_rows = jnp.concatenate([past['ks'], ks_new], axis=1)
        vs_rows = jnp.concatenate([past['vs'], vs_new], axis=1)
        wb = past['win_k'].shape[1]
        kw_all = jnp.concatenate([past['win_k'], kw_new], axis=1)
        vw_all = jnp.concatenate([past['win_v'], vw_new], axis=1)
        kpos = pos[0] - wb + jnp.arange(wb + T, dtype=jnp.int32)
        diff = pos[:, None] - kpos[None, :]
        mask = (diff >= 0) & (diff <= WINDOW)
        o_win, _ = _gqa_attend(qr, kw_all, vw_all, mask[None, :, None, None, :])
        win_k_new = kw_all[:, -wb:]
        win_v_new = vw_all[:, -wb:]
    o_cmp, o_slc = _nsa_compress_select(qn, qr, kc_rows, vc_rows, ks_rows, vs_rows, pos, lw['nsa_cmp_wk'], lw['nsa_cmp_wv'], lw['nsa_k_norm'][0])
    g = jax.nn.sigmoid(z_g.astype(jnp.float32)).reshape(B_, T, KV_HEADS, GQ, 3).astype(z_q.dtype)
    o = g[..., 0:1] * o_cmp + g[..., 1:2] * o_slc + g[..., 2:3] * o_win
    return o.reshape(B_, T, D_C), (kc_new, vc_new, ks_new, vs_new, win_k_new, win_v_new)


def _ffn(x2, g, wg, wu, wd):
    hn = rmsnorm_rows(x2, g)
    act = swiglu_up(hn, wg, wu)
    return mm_residual(act, wd, x2, 0.5)


def _pad_rows(x2, mult=16):
    m = x2.shape[0]
    mp = -(-m // mult) * mult
    return x2 if mp == m else jnp.pad(x2, ((0, mp - m), (0, 0)))


def _prompt_mixers(z3, zg3, pos, lw, wb):
    B_, T, _ = z3.shape
    y_a, tail = conv_prompt(z3, jnp.zeros((B_, CONV_HIST, D_A), jnp.float32), lw['conv_dw'], lw['conv_dw_bias'],
                            lw['conv_ln_g'], lw['conv_ln_b'])
    conv_new = tail[:, CONV_HIST - (CONV_WIDTH - 1):]
    y_b, s_last = s5_prompt(z3, jnp.zeros((B_, 1, 2 * N_STATE), jnp.float32), _s5_params(lw), lw['ssm_d'],
                            wb['ssm_w_glu'], lw['ssm_b_glu'])
    s_re = s_last[:, 0, :N_STATE].reshape(B_, SSM_GROUPS, SSM_STATE)
    s_im = s_last[:, 0, N_STATE:].reshape(B_, SSM_GROUPS, SSM_STATE)
    cosf, sinf = _rope_tables(pos)
    qn, qr, kc, vc, ks, vs, kw, vw, bks, bvs, bkw, bvw = nsa_prep(z3, cosf, sinf, lw['nsa_q_norm'], lw['nsa_k_norm'])
    kcmp, vcmp = nsa_compress(kc, vc, lw['nsa_cmp_wk'], lw['nsa_cmp_wv'], lw['nsa_k_norm'])
    y_c = nsa_attention(qn, qr, kcmp, vcmp, bks, bvs, bkw, bvw, zg3)
    kvshape = (B_, T, KV_HEADS, HEAD_DIM)
    keep = min(WINDOW, T)
    kv_new = (kc.reshape(kvshape), vc.reshape(kvshape), ks.reshape(kvshape), vs.reshape(kvshape),
              kw[:, -keep:].reshape(B_, keep, KV_HEADS, HEAD_DIM), vw[:, -keep:].reshape(B_, keep, KV_HEADS, HEAD_DIM))
    return y_a, y_b, y_c, kv_new + (conv_new, s_re, s_im)


def _sample_mixers(z, z_g, pos, past, lw, wb):
    offs = []
    acc = 0
    for n in (2 * D_A, D_B, D_C, KVD, KVD, KVD, KVD, KVD):
        acc += n
        offs.append(acc)
    z_a, z_b, z_q, z_kc, z_vc, z_ks, z_vs, z_kw, z_vw = jnp.split(z, offs, axis=-1)
    y_a, conv_new = _conformer_conv(z_a, past['conv'], lw['conv_dw'], lw['conv_dw_bias'], lw['conv_ln_g'], lw['conv_ln_b'])
    y_b, s_re, s_im = _s5_mixer(z_b, past['s_re'], past['s_im'], lw, wb)
    y_c, kv_new = _nsa_mixer(z_q, z_kc, z_vc, z_ks, z_vs, z_kw, z_vw, z_g, pos, past, lw)
    return y_a, y_b, y_c, tuple(kv_new) + (conv_new, s_re, s_im)


def _layer_forward(x, pos, past, lw, wb):
    B_, T, _ = x.shape
    m = B_ * T
    x2 = _pad_rows(x.reshape(m, D_MODEL))
    h2 = _ffn(x2, lw['ffn1_norm'], wb['ffn1_w_gate'], wb['ffn1_w_up'], wb['ffn1_w_down'])
    u2 = rmsnorm_rows(h2, lw['mix_norm'])
    z = mm_bias(u2, wb['w_in_main'], jnp.zeros((D_MAIN,), jnp.float32))
    zg = mm_bias(u2, wb['w_in_gate'], jnp.zeros((KV_HEADS * LANE,), jnp.float32))
    if past is None:
        y_a, y_b, y_c, new_state = _prompt_mixers(z.reshape(B_, T, D_MAIN), zg.reshape(B_, T, KV_HEADS * LANE), pos, lw, wb)
    else:
        z_g = zg[:m].reshape(B_, T, KV_HEADS, LANE)[..., :3 * GQ].reshape(B_, T, N_GATE)
        y_a, y_b, y_c, new_state = _sample_mixers(z[:m].reshape(B_, T, D_MAIN), z_g, pos, past, lw, wb)
    zero_d = jnp.zeros((D_MODEL,), jnp.float32)

    def proj(y, w):
        return mm_bias(_pad_rows(y.reshape(m, -1)).astype(jnp.bfloat16), w, zero_d)

    o_a = proj(y_a, wb['conv_w_out'])
    o_b = proj(y_b, wb['ssm_w_out'])
    o_c = proj(y_c, wb['nsa_w_out'])
    gates = mm_bias(u2, wb['merge_w_gate'], lw['merge_b_gate'], act="sigmoid")
    mixed = gates[:, :D_MODEL] * o_a + gates[:, D_MODEL:2 * D_MODEL] * o_b + gates[:, 2 * D_MODEL:] * o_c
    h2 = mm_residual(mixed.astype(jnp.bfloat16), wb['w_out'], h2, 1.0)
    y2 = _ffn(h2, lw['ffn2_norm'], wb['ffn2_w_gate'], wb['ffn2_w_up'], wb['ffn2_w_down'])
    return y2[:m].reshape(B_, T, D_MODEL), new_state


def kernel(x_prompt, x_sample, cache_cmp_k, cache_cmp_v, cache_slc_k, cache_slc_v, cache_win_k, cache_win_v, state_conv, state_ssm_re, state_ssm_im, page_table, ffn1_norm, ffn1_w_gate, ffn1_w_up, ffn1_w_down, mix_norm, w_in, conv_dw, conv_dw_bias, conv_ln_g, conv_ln_b, conv_w_out, ssm_lambda_re, ssm_lambda_im, ssm_log_dt, ssm_b_re, ssm_b_im, ssm_c_re, ssm_c_im, ssm_d, ssm_w_glu, ssm_b_glu, ssm_w_out, nsa_q_norm, nsa_k_norm, nsa_cmp_wk, nsa_cmp_wv, nsa_w_out, merge_w_gate, merge_b_gate, w_out, ffn2_norm, ffn2_w_gate, ffn2_w_up, ffn2_w_down):
    past_len = page_table.shape[1] * PAGE_SIZE
    pos_p = jnp.arange(x_prompt.shape[1], dtype=jnp.int32)
    pos_s = past_len + jnp.arange(x_sample.shape[1], dtype=jnp.int32)
    hp, hs = x_prompt, x_sample
    st_p, st_s = [], []
    bf = jnp.bfloat16
    for l in range(DEPTH):
        lw = dict(ffn1_norm=ffn1_norm[l], mix_norm=mix_norm[l],
                  conv_dw=conv_dw[l], conv_dw_bias=conv_dw_bias[l], conv_ln_g=conv_ln_g[l], conv_ln_b=conv_ln_b[l],
                  ssm_lambda_re=ssm_lambda_re[l], ssm_lambda_im=ssm_lambda_im[l], ssm_log_dt=ssm_log_dt[l],
                  ssm_b_re=ssm_b_re[l], ssm_b_im=ssm_b_im[l], ssm_c_re=ssm_c_re[l], ssm_c_im=ssm_c_im[l],
                  ssm_d=ssm_d[l], ssm_b_glu=ssm_b_glu[l],
                  nsa_q_norm=nsa_q_norm[l], nsa_k_norm=nsa_k_norm[l], nsa_cmp_wk=nsa_cmp_wk[l], nsa_cmp_wv=nsa_cmp_wv[l],
                  merge_b_gate=merge_b_gate[l], ffn2_norm=ffn2_norm[l])
        wb = dict(ffn1_w_gate=ffn1_w_gate[l].astype(bf), ffn1_w_up=ffn1_w_up[l].astype(bf), ffn1_w_down=ffn1_w_down[l].astype(bf),
                  w_in_main=w_in[l][:, :D_MAIN].astype(bf),
                  w_in_gate=jnp.pad(w_in[l][:, D_MAIN:].reshape(D_MODEL, KV_HEADS, 3 * GQ), ((0, 0), (0, 0), (0, LANE - 3 * GQ))).reshape(D_MODEL, KV_HEADS * LANE).astype(bf),
                  conv_w_out=conv_w_out[l].astype(bf), ssm_w_glu=ssm_w_glu[l].astype(bf), ssm_w_out=ssm_w_out[l].astype(bf),
                  nsa_w_out=nsa_w_out[l].astype(bf), merge_w_gate=merge_w_gate[l].astype(bf), w_out=w_out[l].astype(bf),
                  ffn2_w_gate=ffn2_w_gate[l].astype(bf), ffn2_w_up=ffn2_w_up[l].astype(bf), ffn2_w_down=ffn2_w_down[l].astype(bf))
        hp, new_p = _layer_forward(hp, pos_p, None, lw, wb)
        past = dict(kc=_gather_pages(cache_cmp_k[l], page_table), vc=_gather_pages(cache_cmp_v[l], page_table),
                    ks=_gather_pages(cache_slc_k[l], page_table), vs=_gather_pages(cache_slc_v[l], page_table),
                    win_k=cache_win_k[l], win_v=cache_win_v[l], conv=state_conv[l],
                    s_re=state_ssm_re[l], s_im=state_ssm_im[l])
        hs, new_s = _layer_forward(hs, pos_s, past, lw, wb)
        st_p.append(new_p)
        st_s.append(new_s)
    P = [jnp.stack([st[i] for st in st_p], axis=0) for i in range(9)]
    S = [jnp.stack([st[i] for st in st_s], axis=0) for i in range(9)]
    return (hp, hs, P[0], P[1], P[2], P[3], P[4], P[5], P[6], P[7], P[8], S[0], S[1], S[2], S[3], S[4], S[5], S[6], S[7], S[8])
```

```python
import functools
import math

import jax
import jax.numpy as jnp
from jax import lax
from jax.experimental import pallas as pl
from jax.experimental.pallas import tpu as pltpu

D_MODEL = 4096
DEPTH = 4
PAGE_SIZE = 128
D_A = D_MODEL // 4
D_B = D_MODEL // 4
D_C = D_MODEL // 2
HEAD_DIM = 128
N_HEADS = D_C // HEAD_DIM
KV_HEADS = 4
GQ = N_HEADS // KV_HEADS
KVD = KV_HEADS * HEAD_DIM
CONV_WIDTH = 31
SSM_GROUP = 16
SSM_GROUPS = D_B // SSM_GROUP
SSM_STATE = 64
CMP_BLOCK = 32
CMP_STRIDE = 16
SLC_BLOCK = 64
TOP_K = 16
WINDOW = 512
WIN_QB = 128
SLC_Q_CHUNK = 32
ROPE_DIM = HEAD_DIM // 4
ROPE_THETA = 500000.0
D_FF = ((8 * D_MODEL // 3 + 255) // 256) * 256
EPS = 1e-6
SCALE = HEAD_DIM ** -0.5
NEG = -1e30
FORCE = 1e4
D_MAIN = 2 * D_A + D_B + D_C + 6 * KVD
N_GATE = 3 * N_HEADS
LANE = 128

VMEM_LIMIT_BYTES = 56 * 1024 * 1024


def _cparams(*sem):
    return pltpu.CompilerParams(dimension_semantics=sem, vmem_limit_bytes=VMEM_LIMIT_BYTES)


def _rmsnorm_kernel(x_ref, g_ref, o_ref):
    x = x_ref[...]
    ms = jnp.mean(x * x, axis=-1, keepdims=True)
    o_ref[...] = (x * lax.rsqrt(ms + EPS) * g_ref[...]).astype(o_ref.dtype)


def rmsnorm_rows(x, g, out_dtype=jnp.bfloat16):
    m, d = x.shape
    tr = min(m, 256)
    return pl.pallas_call(
        _rmsnorm_kernel,
        grid=(m // tr,),
        in_specs=[pl.BlockSpec((tr, d), lambda i: (i, 0)), pl.BlockSpec((1, d), lambda i: (0, 0))],
        out_specs=pl.BlockSpec((tr, d), lambda i: (i, 0)),
        out_shape=jax.ShapeDtypeStruct((m, d), out_dtype),
        compiler_params=_cparams("parallel"),
        name="rmsnorm_rows",
    )(x, g.reshape(1, d))


def _swiglu_up_kernel(a_ref, wg_ref, wu_ref, o_ref):
    a = a_ref[...]
    hg = jnp.dot(a, wg_ref[...], preferred_element_type=jnp.float32)
    hu = jnp.dot(a, wu_ref[...], preferred_element_type=jnp.float32)
    o_ref[...] = (hg * jax.nn.sigmoid(hg) * hu).astype(o_ref.dtype)


def swiglu_up(a, wg, wu):
    m, k = a.shape
    f = wg.shape[1]
    tm = min(m, 2048)
    tn = 256
    return pl.pallas_call(
        _swiglu_up_kernel,
        grid=(m // tm, f // tn),
        in_specs=[pl.BlockSpec((tm, k), lambda i, j: (i, 0)),
                  pl.BlockSpec((k, tn), lambda i, j: (0, j)),
                  pl.BlockSpec((k, tn), lambda i, j: (0, j))],
        out_specs=pl.BlockSpec((tm, tn), lambda i, j: (i, j)),
        out_shape=jax.ShapeDtypeStruct((m, f), jnp.bfloat16),
        compiler_params=_cparams("parallel", "arbitrary"),
        name="swiglu_up",
    )(a, wg, wu)


def _mm_res_kernel(a_ref, w_ref, r_ref, o_ref, acc_ref, *, nk, scale):
    kk = pl.program_id(2)
    part = jnp.dot(a_ref[...], w_ref[...], preferred_element_type=jnp.float32)

    @pl.when(kk == 0)
    def _():
        acc_ref[...] = part

    @pl.when(kk > 0)
    def _():
        acc_ref[...] += part

    @pl.when(kk == nk - 1)
    def _():
        o_ref[...] = r_ref[...] + scale * acc_ref[...]


def mm_residual(a, w, res, scale):
    m, k = a.shape
    n = w.shape[1]
    tm = min(m, 1024)
    tn = 512
    tk = k if k <= 4096 else k // 2
    nk = k // tk
    return pl.pallas_call(
        functools.partial(_mm_res_kernel, nk=nk, scale=scale),
        grid=(m // tm, n // tn, nk),
        in_specs=[pl.BlockSpec((tm, tk), lambda i, j, kk: (i, kk)),
                  pl.BlockSpec((tk, tn), lambda i, j, kk: (kk, j)),
                  pl.BlockSpec((tm, tn), lambda i, j, kk: (i, j))],
        out_specs=pl.BlockSpec((tm, tn), lambda i, j, kk: (i, j)),
        out_shape=jax.ShapeDtypeStruct((m, n), jnp.float32),
        scratch_shapes=[pltpu.VMEM((tm, tn), jnp.float32)],
        compiler_params=_cparams("parallel", "arbitrary", "arbitrary"),
        name="mm_residual",
    )(a, w, res)


def _mm_bias_kernel(a_ref, w_ref, b_ref, o_ref, *, act):
    y = jnp.dot(a_ref[...], w_ref[...], preferred_element_type=jnp.float32) + b_ref[...]
    if act == "sigmoid":
        y = jax.nn.sigmoid(y)
    o_ref[...] = y.astype(o_ref.dtype)


def mm_bias(a, w, bias, act=None, out_dtype=jnp.float32):
    m, k = a.shape
    n = w.shape[1]
    tm = min(m, 1024)
    tn = 512 if n % 512 == 0 else n
    return pl.pallas_call(
        functools.partial(_mm_bias_kernel, act=act),
        grid=(m // tm, n // tn),
        in_specs=[pl.BlockSpec((tm, k), lambda i, j: (i, 0)),
                  pl.BlockSpec((k, tn), lambda i, j: (0, j)),
                  pl.BlockSpec((1, tn), lambda i, j: (0, j))],
        out_specs=pl.BlockSpec((tm, tn), lambda i, j: (i, j)),
        out_shape=jax.ShapeDtypeStruct((m, n), out_dtype),
        compiler_params=_cparams("parallel", "arbitrary"),
        name="mm_bias",
    )(a, w, bias.reshape(1, n))


def _merge_kernel(u_ref, ya_ref, yb_ref, yc_ref, wga_ref, wgb_ref, wgc_ref, bga_ref, bgb_ref, bgc_ref,
                  wa_ref, wb_ref, wc_ref, o_ref):
    u = u_ref[...]
    acc = None
    for y_ref, w_ref, wg_ref, bg_ref in ((ya_ref, wa_ref, wga_ref, bga_ref), (yb_ref, wb_ref, wgb_ref, bgb_ref),
                                         (yc_ref, wc_ref, wgc_ref, bgc_ref)):
        gate = jax.nn.sigmoid(jnp.dot(u, wg_ref[...], preferred_element_type=jnp.float32) + bg_ref[...])
        term = gate * jnp.dot(y_ref[...], w_ref[...], preferred_element_type=jnp.float32)
        acc = term if acc is None else acc + term
    o_ref[...] = acc.astype(o_ref.dtype)


def gated_merge(u, ya, yb, yc, w_gate, b_gate, wa, wb, wc):
    m, d = u.shape
    tm = min(m, 512)
    tn = 256
    nj = d // tn
    row = lambda width: pl.BlockSpec((tm, width), lambda i, j: (i, 0))

    def gcol(rows, br):
        return pl.BlockSpec((rows, tn), lambda i, j: (0, j + br * nj))

    col = lambda rows: pl.BlockSpec((rows, tn), lambda i, j: (0, j))
    return pl.pallas_call(
        _merge_kernel,
        grid=(m // tm, nj),
        in_specs=[row(d), row(ya.shape[1]), row(yb.shape[1]), row(yc.shape[1]),
                  gcol(d, 0), gcol(d, 1), gcol(d, 2), gcol(1, 0), gcol(1, 1), gcol(1, 2),
                  col(ya.shape[1]), col(yb.shape[1]), col(yc.shape[1])],
        out_specs=pl.BlockSpec((tm, tn), lambda i, j: (i, j)),
        out_shape=jax.ShapeDtypeStruct((m, d), jnp.bfloat16),
        compiler_params=_cparams("parallel", "arbitrary"),
        name="gated_merge",
    )(u, ya, yb, yc, w_gate, w_gate, w_gate, b_gate.reshape(1, 3 * d), b_gate.reshape(1, 3 * d), b_gate.reshape(1, 3 * d),
      wa, wb, wc)


N_STATE = SSM_GROUPS * SSM_STATE
S5_LANE_GROUPS = D_B // LANE
S5_SLAB = N_STATE // S5_LANE_GROUPS
S5_ROWS = 256


def _s5_kernel(z_ref, s0_ref, lr_ref, li_ref, wre_ref, wim_ref, cr_ref, ci_ref, d_ref, wglu_ref, bglu_ref,
               y_ref, slast_ref, bre_ref, bim_ref, sre_ref, sim_ref, yacc_ref):
    tt = z_ref.shape[0]

    @pl.when(pl.program_id(1) == 0)
    def _():
        sre_ref[...] = s0_ref[:, :N_STATE]
        sim_ref[...] = s0_ref[:, N_STATE:]

    x = z_ref[...]
    xb = x.astype(jnp.bfloat16)
    for j in range(S5_LANE_GROUPS):
        xj = xb[:, j * LANE:(j + 1) * LANE]
        cs = slice(j * S5_SLAB, (j + 1) * S5_SLAB)
        bre_ref[:, cs] = jnp.dot(xj, wre_ref[j], preferred_element_type=jnp.float32)
        bim_ref[:, cs] = jnp.dot(xj, wim_ref[j], preferred_element_type=jnp.float32)

    for c in range(S5_LANE_GROUPS):
        cs = slice(c * S5_SLAB, (c + 1) * S5_SLAB)
        lr = lr_ref[:, cs]
        li = li_ref[:, cs]

        def step(t, carry, cs=cs, lr=lr, li=li):
            sr, si = carry
            nr = lr * sr - li * si + bre_ref[pl.ds(t, 1), cs]
            ni = lr * si + li * sr + bim_ref[pl.ds(t, 1), cs]
            bre_ref[pl.ds(t, 1), cs] = nr
            bim_ref[pl.ds(t, 1), cs] = ni
            return nr, ni

        sr, si = lax.fori_loop(0, tt, step, (sre_ref[:, cs], sim_ref[:, cs]), unroll=8)
        sre_ref[:, cs] = sr
        sim_ref[:, cs] = si

    for j in range(S5_LANE_GROUPS):
        cs = slice(j * S5_SLAB, (j + 1) * S5_SLAB)
        yj = jnp.dot(bre_ref[:, cs].astype(jnp.bfloat16), cr_ref[j], preferred_element_type=jnp.float32)
        yj = yj + jnp.dot(bim_ref[:, cs].astype(jnp.bfloat16), ci_ref[j], preferred_element_type=jnp.float32)
        yacc_ref[:, j * LANE:(j + 1) * LANE] = yj
    y = jax.nn.gelu(yacc_ref[...] + d_ref[...] * x)
    gl = jnp.dot(y.astype(jnp.bfloat16), wglu_ref[...], preferred_element_type=jnp.float32) + bglu_ref[...]
    y_ref[...] = (y * jax.nn.sigmoid(gl)).astype(y_ref.dtype)
    slast_ref[:, :N_STATE] = sre_ref[...]
    slast_ref[:, N_STATE:] = sim_ref[...]


def _s5_params(lw):
    lam = lax.complex(lw['ssm_lambda_re'], lw['ssm_lambda_im'])
    dt = jnp.exp(lw['ssm_log_dt'])[:, None]
    lam_bar = jnp.exp(lam * dt)
    b_bar = ((lam_bar - 1.0) / lam)[..., None] * lax.complex(lw['ssm_b_re'], lw['ssm_b_im'])
    gpl = LANE // SSM_GROUP
    eye = jnp.eye(gpl, dtype=jnp.float32)

    def bdiag_in(b):
        b4 = b.reshape(S5_LANE_GROUPS, gpl, SSM_STATE, SSM_GROUP)
        m = jnp.einsum('jgpc,gh->jgchp', b4, eye)
        return m.reshape(S5_LANE_GROUPS, LANE, S5_SLAB).astype(jnp.bfloat16)

    def bdiag_out(cm):
        c4 = cm.reshape(S5_LANE_GROUPS, gpl, SSM_GROUP, SSM_STATE)
        m = jnp.einsum('jgcp,gh->jgphc', c4, eye)
        return m.reshape(S5_LANE_GROUPS, S5_SLAB, LANE).astype(jnp.bfloat16)

    return dict(lr=jnp.real(lam_bar).reshape(1, N_STATE), li=jnp.imag(lam_bar).reshape(1, N_STATE),
                wre=bdiag_in(jnp.real(b_bar)), wim=bdiag_in(jnp.imag(b_bar)),
                cr=bdiag_out(lw['ssm_c_re']), ci=bdiag_out(-lw['ssm_c_im']))


def s5_prompt(z3, s0, sp, d, wglu, bglu):
    b, t, _ = z3.shape
    tt = min(t, S5_ROWS)
    zcol = (2 * D_A) // D_B
    const3 = lambda bi, ti: (0, 0, 0)
    const2 = lambda bi, ti: (0, 0)
    return pl.pallas_call(
        _s5_kernel,
        grid=(b, t // tt),
        in_specs=[pl.BlockSpec((None, tt, D_B), lambda bi, ti: (bi, ti, zcol)),
                  pl.BlockSpec((None, 1, 2 * N_STATE), lambda bi, ti: (bi, 0, 0)),
                  pl.BlockSpec((1, N_STATE), const2), pl.BlockSpec((1, N_STATE), const2),
                  pl.BlockSpec((S5_LANE_GROUPS, LANE, S5_SLAB), const3), pl.BlockSpec((S5_LANE_GROUPS, LANE, S5_SLAB), const3),
                  pl.BlockSpec((S5_LANE_GROUPS, S5_SLAB, LANE), const3), pl.BlockSpec((S5_LANE_GROUPS, S5_SLAB, LANE), const3),
                  pl.BlockSpec((1, D_B), const2), pl.BlockSpec((D_B, D_B), const2), pl.BlockSpec((1, D_B), const2)],
        out_specs=[pl.BlockSpec((None, tt, D_B), lambda bi, ti: (bi, ti, 0)),
                   pl.BlockSpec((None, 1, 2 * N_STATE), lambda bi, ti: (bi, 0, 0))],
        out_shape=[jax.ShapeDtypeStruct((b, t, D_B), jnp.bfloat16),
                   jax.ShapeDtypeStruct((b, 1, 2 * N_STATE), jnp.float32)],
        scratch_shapes=[pltpu.VMEM((tt, N_STATE), jnp.float32), pltpu.VMEM((tt, N_STATE), jnp.float32),
                        pltpu.VMEM((1, N_STATE), jnp.float32), pltpu.VMEM((1, N_STATE), jnp.float32),
                        pltpu.VMEM((tt, D_B), jnp.float32)],
        compiler_params=_cparams("parallel", "arbitrary"),
        name="s5_prompt",
    )(z3, s0, sp['lr'], sp['li'], sp['wre'], sp['wim'], sp['cr'], sp['ci'], d.reshape(1, D_B), wglu, bglu.reshape(1, D_B))


def _s5_step_kernel(x_ref, sre_ref, sim_ref, lr_ref, li_ref, wre_ref, wim_ref, cr_ref, ci_ref, d_ref, wglu_ref, bglu_ref,
                    y_ref, nre_ref, nim_ref):
    x = x_ref[...]
    xb = x.astype(jnp.bfloat16)
    cols = []
    for j in range(S5_LANE_GROUPS):
        xj = xb[:, j * LANE:(j + 1) * LANE]
        cs = slice(j * S5_SLAB, (j + 1) * S5_SLAB)
        sr, si, lr, li = sre_ref[:, cs], sim_ref[:, cs], lr_ref[:, cs], li_ref[:, cs]
        nr = lr * sr - li * si + jnp.dot(xj, wre_ref[j], preferred_element_type=jnp.float32)
        ni = lr * si + li * sr + jnp.dot(xj, wim_ref[j], preferred_element_type=jnp.float32)
        nre_ref[:, cs] = nr
        nim_ref[:, cs] = ni
        cols.append(jnp.dot(nr.astype(jnp.bfloat16), cr_ref[j], preferred_element_type=jnp.float32)
                    + jnp.dot(ni.astype(jnp.bfloat16), ci_ref[j], preferred_element_type=jnp.float32))
    y = jax.nn.gelu(jnp.concatenate(cols, axis=1) + d_ref[...] * x)
    gl = jnp.dot(y.astype(jnp.bfloat16), wglu_ref[...], preferred_element_type=jnp.float32) + bglu_ref[...]
    y_ref[...] = (y * jax.nn.sigmoid(gl)).astype(y_ref.dtype)


def s5_step(x, s_re, s_im, sp, d, wglu, bglu):
    r = x.shape[0]
    st = jax.ShapeDtypeStruct((r, N_STATE), jnp.float32)
    return pl.pallas_call(
        _s5_step_kernel,
        out_shape=[jax.ShapeDtypeStruct((r, D_B), jnp.bfloat16), st, st],
        compiler_params=pltpu.CompilerParams(vmem_limit_bytes=VMEM_LIMIT_BYTES),
        name="s5_step",
    )(x, s_re, s_im, sp['lr'], sp['li'], sp['wre'], sp['wim'], sp['cr'], sp['ci'], d.reshape(1, D_B), wglu, bglu.reshape(1, D_B))


def _conv_step_kernel(zl_ref, zg_ref, st_ref, dw_ref, dwb_ref, lng_ref, lnb_ref, y_ref, new_ref):
    a = zl_ref[...] * jax.nn.sigmoid(zg_ref[...])
    hist = CONV_WIDTH - 1
    acc = a * dw_ref[hist:hist + 1, :] + dwb_ref[...]
    for k in range(hist):
        acc = acc + st_ref[k] * dw_ref[k:k + 1, :]
    mu = jnp.mean(acc, axis=-1, keepdims=True)
    yc = acc - mu
    var = jnp.mean(yc * yc, axis=-1, keepdims=True)
    yn = yc * lax.rsqrt(var + EPS) * lng_ref[...] + lnb_ref[...]
    y_ref[...] = (yn * jax.nn.sigmoid(yn)).astype(y_ref.dtype)
    new_ref[0:hist - 1] = st_ref[1:hist]
    new_ref[hist - 1] = a


def conv_step(zl, zg, hist, dw, dwb, lng, lnb):
    r = zl.shape[0]
    return pl.pallas_call(
        _conv_step_kernel,
        out_shape=[jax.ShapeDtypeStruct((r, D_A), jnp.bfloat16), jax.ShapeDtypeStruct(hist.shape, jnp.float32)],
        compiler_params=pltpu.CompilerParams(vmem_limit_bytes=VMEM_LIMIT_BYTES),
        name="conv_step",
    )(zl, zg, hist, dw, dwb.reshape(1, D_A), lng.reshape(1, D_A), lnb.reshape(1, D_A))


CONV_HIST = 32
CONV_ROWS = 256


def _conv_kernel(zl_ref, zg_ref, buf_ref, dw_ref, dwb_ref, lng_ref, lnb_ref, y_ref, tail_ref, xpad_ref):
    tt = zl_ref.shape[0]

    @pl.when(pl.program_id(1) == 0)
    def _():
        xpad_ref[0:CONV_HIST, :] = buf_ref[...]

    xpad_ref[CONV_HIST:CONV_HIST + tt, :] = zl_ref[...] * jax.nn.sigmoid(zg_ref[...])
    off = CONV_HIST - (CONV_WIDTH - 1)
    acc = xpad_ref[off:off + tt, :] * dw_ref[0:1, :]
    for k in range(1, CONV_WIDTH):
        acc = acc + xpad_ref[off + k:off + k + tt, :] * dw_ref[k:k + 1, :]
    y = acc + dwb_ref[...]
    mu = jnp.mean(y, axis=-1, keepdims=True)
    yc = y - mu
    var = jnp.mean(yc * yc, axis=-1, keepdims=True)
    yn = yc * lax.rsqrt(var + EPS) * lng_ref[...] + lnb_ref[...]
    y_ref[...] = (yn * jax.nn.sigmoid(yn)).astype(y_ref.dtype)
    tail = xpad_ref[tt:tt + CONV_HIST, :]
    tail_ref[...] = tail
    xpad_ref[0:CONV_HIST, :] = tail


def conv_prompt(z3, buf, dw, dwb, lng, lnb):
    b, t, _ = z3.shape
    tt = min(t, CONV_ROWS)
    const2 = lambda bi, ti: (0, 0)
    return pl.pallas_call(
        _conv_kernel,
        grid=(b, t // tt),
        in_specs=[pl.BlockSpec((None, tt, D_A), lambda bi, ti: (bi, ti, 0)),
                  pl.BlockSpec((None, tt, D_A), lambda bi, ti: (bi, ti, 1)),
                  pl.BlockSpec((None, CONV_HIST, D_A), lambda bi, ti: (bi, 0, 0)),
                  pl.BlockSpec((CONV_HIST, D_A), const2), pl.BlockSpec((1, D_A), const2),
                  pl.BlockSpec((1, D_A), const2), pl.BlockSpec((1, D_A), const2)],
        out_specs=[pl.BlockSpec((None, tt, D_A), lambda bi, ti: (bi, ti, 0)),
                   pl.BlockSpec((None, CONV_HIST, D_A), lambda bi, ti: (bi, 0, 0))],
        out_shape=[jax.ShapeDtypeStruct((b, t, D_A), jnp.bfloat16),
                   jax.ShapeDtypeStruct((b, CONV_HIST, D_A), jnp.float32)],
        scratch_shapes=[pltpu.VMEM((tt + CONV_HIST, D_A), jnp.float32)],
        compiler_params=_cparams("parallel", "arbitrary"),
        name="conv_prompt",
    )(z3, z3, buf, jnp.pad(dw, ((0, CONV_HIST - CONV_WIDTH), (0, 0))), dwb.reshape(1, D_A), lng.reshape(1, D_A), lnb.reshape(1, D_A))


NSA_ROWS = 256
SLC_KEY_EXT = 512
Q_COL0 = 2 * D_A + D_B
KV_COL0 = Q_COL0 + D_C
CMP_PER_ROW = 16 * KVD


def _head_rmsnorm(x, g):
    ms = jnp.mean(x * x, axis=-1, keepdims=True)
    return x * lax.rsqrt(ms + EPS) * g


def _rope(x, cosf, sinf, lane):
    half = ROPE_DIM // 2
    rot = jnp.where(lane < half, pltpu.roll(x, HEAD_DIM - half, 1), pltpu.roll(x, half, 1))
    return x * cosf + rot * sinf


def _nsa_prep_kernel(zq0_ref, zq1_ref, kc_ref, vc_ref, ks_ref, vs_ref, kw_ref, vw_ref, cos_ref, sin_ref, qg_ref, kg_ref,
                     qn_ref, qr_ref, okc_ref, ovc_ref, oks_ref, ovs_ref, okw_ref, ovw_ref,
                     bks_ref, bvs_ref, bkw_ref, bvw_ref):
    tt = cos_ref.shape[0]
    cosf = cos_ref[...]
    sinf = sin_ref[...]
    lane = lax.broadcasted_iota(jnp.int32, (tt, HEAD_DIM), 1)
    qg = qg_ref[...]
    heads_per_ref = N_HEADS // 2
    for h in range(N_HEADS):
        src = zq0_ref if h < heads_per_ref else zq1_ref
        hh = h % heads_per_ref
        xn = _head_rmsnorm(src[:, hh * HEAD_DIM:(hh + 1) * HEAD_DIM], qg)
        qn_ref[:, h * HEAD_DIM:(h + 1) * HEAD_DIM] = xn.astype(qn_ref.dtype)
        qr_ref[:, h * HEAD_DIM:(h + 1) * HEAD_DIM] = _rope(xn, cosf, sinf, lane).astype(qr_ref.dtype)
    okc_ref[...] = kc_ref[...]
    ovc_ref[...] = vc_ref[...]
    vs = vs_ref[...]
    ovs_ref[...] = vs
    bvs_ref[...] = vs.astype(bvs_ref.dtype)
    vw = vw_ref[...]
    ovw_ref[...] = vw
    bvw_ref[...] = vw.astype(bvw_ref.dtype)
    for h in range(KV_HEADS):
        hs = slice(h * HEAD_DIM, (h + 1) * HEAD_DIM)
        ks = _rope(_head_rmsnorm(ks_ref[:, hs], kg_ref[1:2, :]), cosf, sinf, lane)
        oks_ref[:, hs] = ks
        bks_ref[:, hs] = ks.astype(bks_ref.dtype)
        kw = _rope(_head_rmsnorm(kw_ref[:, hs], kg_ref[2:3, :]), cosf, sinf, lane)
        okw_ref[:, hs] = kw
        bkw_ref[:, hs] = kw.astype(bkw_ref.dtype)


def nsa_prep(z3, cosf, sinf, q_norm, k_norm):
    b, t, _ = z3.shape
    tt = min(t, NSA_ROWS)
    half_q = D_C // 2
    kvb = KV_COL0 // KVD

    def zcol(width, idx):
        return pl.BlockSpec((None, tt, width), lambda bi, ti: (bi, ti, idx))

    row = lambda width: pl.BlockSpec((None, tt, width), lambda bi, ti: (bi, ti, 0))
    const2 = lambda bi, ti: (0, 0)
    f32, bf16 = jnp.float32, jnp.bfloat16
    kv_f32 = jax.ShapeDtypeStruct((b, t, KVD), f32)
    kv_bf = jax.ShapeDtypeStruct((b, t, KVD), bf16)
    return pl.pallas_call(
        _nsa_prep_kernel,
        grid=(b, t // tt),
        in_specs=[zcol(half_q, Q_COL0 // half_q), zcol(half_q, Q_COL0 // half_q + 1)]
                 + [zcol(KVD, kvb + i) for i in range(6)]
                 + [pl.BlockSpec((tt, HEAD_DIM), lambda bi, ti: (ti, 0)), pl.BlockSpec((tt, HEAD_DIM), lambda bi, ti: (ti, 0)),
                    pl.BlockSpec((1, HEAD_DIM), const2), pl.BlockSpec((3, HEAD_DIM), const2)],
        out_specs=[row(D_C), row(D_C)] + [row(KVD)] * 10,
        out_shape=[jax.ShapeDtypeStruct((b, t, D_C), bf16), jax.ShapeDtypeStruct((b, t, D_C), bf16)]
                  + [kv_f32] * 6 + [kv_bf] * 4,
        compiler_params=_cparams("parallel", "arbitrary"),
        name="nsa_prep",
    )(z3, z3, z3, z3, z3, z3, z3, z3, cosf, sinf, q_norm.reshape(1, HEAD_DIM), k_norm)


def _compress_kernel(xk_ref, xks_ref, xv_ref, xvs_ref, wk_ref, wv_ref, kg_ref, kc_ref, vc_ref):
    half = CMP_BLOCK // 2

    def pooled(x_ref, xs_ref, w_ref):
        acc = x_ref[:, 0:KVD] * w_ref[0:1, :] + xs_ref[:, 0:KVD] * w_ref[half:half + 1, :]
        for i in range(1, half):
            cs = slice(i * KVD, (i + 1) * KVD)
            acc = acc + x_ref[:, cs] * w_ref[i:i + 1, :] + xs_ref[:, cs] * w_ref[half + i:half + i + 1, :]
        return acc

    kc = pooled(xk_ref, xks_ref, wk_ref)
    vc = pooled(xv_ref, xvs_ref, wv_ref)
    vc_ref[...] = vc.astype(vc_ref.dtype)
    for h in range(KV_HEADS):
        hs = slice(h * HEAD_DIM, (h + 1) * HEAD_DIM)
        kc_ref[:, hs] = _head_rmsnorm(kc[:, hs], kg_ref[0:1, :]).astype(kc_ref.dtype)


def nsa_compress(kc_rows, vc_rows, cmp_wk, cmp_wv, k_norm):
    b, t, _ = kc_rows.shape
    nb = t // CMP_STRIDE
    rb = min(nb, 64)

    def views(x):
        x2 = x.reshape(b, nb, CMP_PER_ROW)
        return x2, jnp.concatenate([x2[:, 1:], jnp.zeros((b, 1, CMP_PER_ROW), x.dtype)], axis=1)

    xk, xks = views(kc_rows)
    xv, xvs = views(vc_rows)
    blk = pl.BlockSpec((None, rb, CMP_PER_ROW), lambda bi, ri: (bi, ri, 0))
    const2 = lambda bi, ri: (0, 0)
    out = pl.BlockSpec((None, rb, KVD), lambda bi, ri: (bi, ri, 0))
    return pl.pallas_call(
        _compress_kernel,
        grid=(b, nb // rb),
        in_specs=[blk, blk, blk, blk, pl.BlockSpec((CMP_BLOCK, KVD), const2), pl.BlockSpec((CMP_BLOCK, KVD), const2),
                  pl.BlockSpec((3, HEAD_DIM), const2)],
        out_specs=[out, out],
        out_shape=[jax.ShapeDtypeStruct((b, nb, KVD), jnp.bfloat16)] * 2,
        compiler_params=_cparams("parallel", "arbitrary"),
        name="nsa_compress",
    )(xk, xks, xv, xvs, cmp_wk.transpose(1, 0, 2).reshape(CMP_BLOCK, KVD), cmp_wv.transpose(1, 0, 2).reshape(CMP_BLOCK, KVD), k_norm)


def _dot_nt(a, b):
    return lax.dot_general(a, b, (((1,), (1,)), ((), ())), preferred_element_type=jnp.float32)


def _softmax_rows(s, mask):
    s = jnp.where(mask, s, NEG)
    m = jnp.max(s, axis=-1, keepdims=True)
    e = jnp.where(mask, jnp.exp(s - m), 0.0)
    return e / jnp.maximum(jnp.sum(e, axis=-1, keepdims=True), 1e-30)


def _nsa_attn_kernel(qn_ref, qr_ref, kc_ref, vc_ref, ks_ref, vs_ref, kw_ref, vw_ref, zg_ref, ov_ref, ex_ref, o_ref, oslc_ref):
    tq = qn_ref.shape[0]
    t_all = ks_ref.shape[0]
    n_cmp_pad = kc_ref.shape[0]
    t0 = pl.program_id(2) * tq
    f32, bf16 = jnp.float32, jnp.bfloat16

    qn = jnp.concatenate([qn_ref[:, g * HEAD_DIM:(g + 1) * HEAD_DIM] for g in range(GQ)], axis=0)
    s = _dot_nt(qn, kc_ref[...]) * SCALE
    rows = lax.broadcasted_iota(jnp.int32, (GQ * tq, n_cmp_pad), 0)
    tpos_c = t0 + (rows & (tq - 1))
    jblk = lax.broadcasted_iota(jnp.int32, (GQ * tq, n_cmp_pad), 1)
    p = _softmax_rows(s, jblk * CMP_STRIDE + (CMP_BLOCK - 1) <= tpos_c)
    o_cmp = jnp.dot(p.astype(bf16), vc_ref[...], preferred_element_type=f32)
    psum = p[0:tq]
    for g in range(1, GQ):
        psum = psum + p[g * tq:(g + 1) * tq]

    imp = jnp.dot(psum, ov_ref[...], preferred_element_type=f32, precision=lax.Precision.HIGHEST)
    n_lanes = ov_ref.shape[1]
    n_slc = t_all // SLC_BLOCK
    nb = lax.broadcasted_iota(jnp.int32, (tq, n_lanes), 1)
    tpos = t0 + lax.broadcasted_iota(jnp.int32, (tq, n_lanes), 0)
    cur = tpos // SLC_BLOCK
    forced = (nb == 0) | (nb == cur) | (nb == cur - 1)
    valid = nb <= cur
    score = jnp.where(valid, jnp.where(forced, FORCE, imp), -FORCE)
    rank = jnp.zeros((tq, n_lanes), f32)
    for m in range(n_slc):
        col = score[:, m:m + 1]
        ahead = (col > score) | ((col == score) & (nb > m))
        rank = rank + jnp.where(ahead, 1.0, 0.0)
    sel = jnp.where((rank < float(min(TOP_K, n_slc))) & valid, 1.0, 0.0)
    sel_bf = sel.astype(bf16)

    def attend(qr, k, v, bias):
        sc = _dot_nt(qr, k) * SCALE + bias
        e = jnp.exp(sc - jnp.max(sc, axis=-1, keepdims=True))
        inv = 1.0 / jnp.sum(e, axis=-1, keepdims=True)
        return jnp.dot(e.astype(bf16), v, preferred_element_type=f32) * inv

    ext = min(SLC_KEY_EXT, t_all)
    for v in range(t_all // ext):
        nk = (v + 1) * ext

        @pl.when(t0 // ext == v)
        def _(nk=nk):
            allowed = jnp.dot(sel_bf, ex_ref[:, 0:nk], preferred_element_type=f32)
            kpos = lax.broadcasted_iota(jnp.int32, (tq, nk), 1)
            tq_pos = t0 + lax.broadcasted_iota(jnp.int32, (tq, nk), 0)
            bias_s = jnp.where((allowed > 0.5) & (kpos <= tq_pos), 0.0, NEG)
            ks = ks_ref[0:nk, :]
            vs = vs_ref[0:nk, :]
            for g in range(GQ):
                oslc_ref[g * tq:(g + 1) * tq, :] = attend(qr_ref[:, g * HEAD_DIM:(g + 1) * HEAD_DIM], ks, vs, bias_s)

    span = min(WINDOW + tq, t_all)
    w0 = pl.multiple_of(jnp.maximum(t0 + tq - span, 0), tq)
    kwin = kw_ref[pl.ds(w0, span), :]
    vwin = vw_ref[pl.ds(w0, span), :]
    wpos = w0 + lax.broadcasted_iota(jnp.int32, (tq, span), 1)
    wdiff = t0 + lax.broadcasted_iota(jnp.int32, (tq, span), 0) - wpos
    bias_w = jnp.where((wdiff >= 0) & (wdiff <= WINDOW), 0.0, NEG)

    gate = jax.nn.sigmoid(zg_ref[...])
    for g in range(GQ):
        hs = slice(g * HEAD_DIM, (g + 1) * HEAD_DIM)
        o_win = attend(qr_ref[:, hs], kwin, vwin, bias_w)
        o = (gate[:, 3 * g:3 * g + 1] * o_cmp[g * tq:(g + 1) * tq]
             + gate[:, 3 * g + 1:3 * g + 2] * oslc_ref[g * tq:(g + 1) * tq, :] + gate[:, 3 * g + 2:3 * g + 3] * o_win)
        o_ref[:, hs] = o.astype(o_ref.dtype)


def nsa_attention(qn, qr, kcmp, vcmp, ks, vs, kw, vw, zg):
    b, t, _ = qn.shape
    tq = min(t, WIN_QB)
    n_cmp_pad = kcmp.shape[1]
    n_slc = t // SLC_BLOCK
    starts = jnp.arange(n_cmp_pad, dtype=jnp.int32) * CMP_STRIDE
    blk_start = jnp.arange(LANE, dtype=jnp.int32) * SLC_BLOCK
    overlap = ((starts[:, None] < blk_start[None, :] + SLC_BLOCK) & (starts[:, None] + CMP_BLOCK > blk_start[None, :])
               & (jnp.arange(LANE)[None, :] < n_slc) & (starts[:, None] + CMP_BLOCK <= t)).astype(jnp.float32)
    expand = (jnp.arange(LANE, dtype=jnp.int32)[:, None] == (jnp.arange(t, dtype=jnp.int32) // SLC_BLOCK)[None, :]).astype(jnp.bfloat16)
    qspec = pl.BlockSpec((None, tq, GQ * HEAD_DIM), lambda bi, hi, qi: (bi, qi, hi))
    kvspec = lambda rows: pl.BlockSpec((None, rows, HEAD_DIM), lambda bi, hi, qi: (bi, 0, hi))
    const2 = lambda bi, hi, qi: (0, 0)
    return pl.pallas_call(
        _nsa_attn_kernel,
        grid=(b, KV_HEADS, t // tq),
        in_specs=[qspec, qspec, kvspec(n_cmp_pad), kvspec(n_cmp_pad), kvspec(t), kvspec(t), kvspec(t), kvspec(t),
                  pl.BlockSpec((None, tq, LANE), lambda bi, hi, qi: (bi, qi, hi)),
                  pl.BlockSpec((n_cmp_pad, LANE), const2), pl.BlockSpec((LANE, t), const2)],
        out_specs=qspec,
        out_shape=jax.ShapeDtypeStruct((b, t, D_C), jnp.bfloat16),
        scratch_shapes=[pltpu.VMEM((GQ * tq, HEAD_DIM), jnp.float32)],
        compiler_params=_cparams("parallel", "parallel", "arbitrary"),
        name="nsa_attention",
    )(qn, qr, kcmp, vcmp, ks, vs, kw, vw, zg, overlap, expand)


def _rope_tables(pos):
    half = ROPE_DIM // 2
    inv_freq = ROPE_THETA ** (-jnp.arange(half, dtype=jnp.float32) / half)
    ang = pos.astype(jnp.float32)[:, None] * inv_freq[None, :]
    cos, sin = jnp.cos(ang), jnp.sin(ang)
    t = pos.shape[0]
    cosf = jnp.concatenate([cos, cos, jnp.ones((t, HEAD_DIM - ROPE_DIM), jnp.float32)], axis=1)
    sinf = jnp.concatenate([-sin, sin, jnp.zeros((t, HEAD_DIM - ROPE_DIM), jnp.float32)], axis=1)
    return cosf, sinf


def _rmsnorm(x, g):
    xf = x.astype(jnp.float32)
    y = xf * lax.rsqrt(jnp.mean(xf * xf, axis=-1, keepdims=True) + EPS)
    return (y * g.astype(jnp.float32)).astype(x.dtype)


def _partial_rope(x, pos):
    half = ROPE_DIM // 2
    inv_freq = ROPE_THETA ** (-jnp.arange(half, dtype=jnp.float32) / half)
    ang = pos.astype(jnp.float32)[:, None] * inv_freq[None, :]
    cos = jnp.cos(ang)[None, :, None, :]
    sin = jnp.sin(ang)[None, :, None, :]
    xr = x[..., :ROPE_DIM].astype(jnp.float32)
    x1, x2 = xr[..., :half], xr[..., half:]
    rot = jnp.concatenate([x1 * cos - x2 * sin, x2 * cos + x1 * sin], axis=-1)
    return jnp.concatenate([rot.astype(x.dtype), x[..., ROPE_DIM:]], axis=-1)


def _masked_softmax(s, mask):
    s = jnp.where(mask, s, NEG)
    m = jnp.max(s, axis=-1, keepdims=True)
    e = jnp.where(mask, jnp.exp(s - m), 0.0)
    return e / jnp.maximum(jnp.sum(e, axis=-1, keepdims=True), 1e-30)


def _gqa_attend(q, k, v, mask):
    s = jnp.einsum('...qkgd,...skd->...qkgs', q, k).astype(jnp.float32) * SCALE
    p = _masked_softmax(s, mask)
    return jnp.einsum('...qkgs,...skd->...qkgd', p.astype(v.dtype), v), p


def _select_attend(q, sel, ok, pos, kg, vg):
    s = jnp.einsum('btkgd,btknsd->btkgns', q, kg).astype(jnp.float32) * SCALE
    tok = sel[..., None] * SLC_BLOCK + jnp.arange(SLC_BLOCK, dtype=sel.dtype)
    mask = (tok <= pos[None, :, None, None, None]) & ok[..., None]
    shp = s.shape
    p = _masked_softmax(s.reshape(*shp[:4], -1), mask.reshape(*mask.shape[:3], 1, -1)).reshape(shp)
    return jnp.einsum('btkgns,btknsd->btkgd', p.astype(vg.dtype), vg)


def _compress_paged_kernel(pt_ref, kcur_ref, knext_ref, vcur_ref, vnext_ref, wk_ref, wv_ref, kg_ref, pool_ref, pools_ref,
                           kc_ref, vc_ref):
    del pt_ref
    hp = lax.Precision.HIGHEST

    def pooled(cur_ref, next_ref, w_ref):
        x = cur_ref[...]
        xcat = jnp.concatenate([x, next_ref[...]], axis=0)
        lo = jnp.dot(pool_ref[...], x * w_ref[0:PAGE_SIZE, :], preferred_element_type=jnp.float32, precision=hp)
        hi = jnp.dot(pools_ref[...], xcat * w_ref[PAGE_SIZE:3 * PAGE_SIZE, :], preferred_element_type=jnp.float32, precision=hp)
        return lo + hi

    kc = pooled(kcur_ref, knext_ref, wk_ref)
    vc_ref[...] = pooled(vcur_ref, vnext_ref, wv_ref)
    for h in range(KV_HEADS):
        hs = slice(h * HEAD_DIM, (h + 1) * HEAD_DIM)
        kc_ref[:, hs] = _head_rmsnorm(kc[:, hs], kg_ref[0:1, :])


def nsa_compress_paged(pool_k, pool_v, page_table, cmp_wk, cmp_wv, k_norm):
    npool = pool_k.shape[0]
    b, npages = page_table.shape
    rpp = PAGE_SIZE // CMP_STRIDE
    half = CMP_BLOCK // 2
    r1 = jnp.arange(PAGE_SIZE, dtype=jnp.int32)
    r2 = jnp.arange(2 * PAGE_SIZE, dtype=jnp.int32)
    m = jnp.arange(rpp, dtype=jnp.int32)[:, None]

    def wtile(w):
        w2 = w.transpose(1, 0, 2).reshape(CMP_BLOCK, KVD)
        return jnp.concatenate([w2[r1 % half], w2[half + r2 % half]], axis=0)

    pool = (r1[None, :] // CMP_STRIDE == m).astype(jnp.float32)
    pools = (r2[None, :] // CMP_STRIDE == m + 1).astype(jnp.float32)
    cur = pl.BlockSpec((None, PAGE_SIZE, KVD), lambda bi, pi, pt: (pt[bi, pi], 0, 0))
    nxt = pl.BlockSpec((None, PAGE_SIZE, KVD), lambda bi, pi, pt: (pt[bi, jnp.minimum(pi + 1, npages - 1)], 0, 0))
    const = lambda bi, pi, pt: (0, 0)
    out = pl.BlockSpec((None, rpp, KVD), lambda bi, pi, pt: (bi, pi, 0))
    grid_spec = pltpu.PrefetchScalarGridSpec(
        num_scalar_prefetch=1, grid=(b, npages),
        in_specs=[cur, nxt, cur, nxt, pl.BlockSpec((3 * PAGE_SIZE, KVD), const), pl.BlockSpec((3 * PAGE_SIZE, KVD), const),
                  pl.BlockSpec((3, HEAD_DIM), const), pl.BlockSpec((rpp, PAGE_SIZE), const), pl.BlockSpec((rpp, 2 * PAGE_SIZE), const)],
        out_specs=[out, out])
    pk = pool_k.reshape(npool, PAGE_SIZE, KVD)
    pv = pool_v.reshape(npool, PAGE_SIZE, KVD)
    return pl.pallas_call(
        _compress_paged_kernel,
        grid_spec=grid_spec,
        out_shape=[jax.ShapeDtypeStruct((b, npages * rpp, KVD), jnp.float32)] * 2,
        compiler_params=_cparams("parallel", "arbitrary"),
        name="nsa_compress_paged",
    )(page_table, pk, pk, pv, pv, wtile(cmp_wk), wtile(cmp_wv), k_norm, pool, pools)


def _nsa_compress_select(qn, qr, ks_new, vs_new, pos, past, lw):
    B_, T = qn.shape[:2]
    page_table = past['page_table']
    past_len = page_table.shape[1] * PAGE_SIZE
    L = past_len + T
    n_cmp = (L - CMP_BLOCK) // CMP_STRIDE + 1
    assert T == 1 and past_len % SLC_BLOCK == 0 and (n_cmp - 1) * CMP_STRIDE + CMP_BLOCK <= past_len
    starts = jnp.arange(n_cmp, dtype=jnp.int32) * CMP_STRIDE
    kcmp, vcmp = nsa_compress_paged(past['pool_kc'], past['pool_vc'], page_table, lw['nsa_cmp_wk'], lw['nsa_cmp_wv'], lw['nsa_k_norm'])
    k_cmp = kcmp[:, :n_cmp].reshape(B_, n_cmp, KV_HEADS, HEAD_DIM)
    v_cmp = vcmp[:, :n_cmp].reshape(B_, n_cmp, KV_HEADS, HEAD_DIM)
    mask_c = (starts + CMP_BLOCK - 1)[None, :] <= pos[:, None]
    o_cmp, p_cmp = _gqa_attend(qn, k_cmp, v_cmp, mask_c[None, :, None, None, :])
    n_slc = -(-L // SLC_BLOCK)
    blk = jnp.arange(n_slc, dtype=jnp.int32)
    blk_start = blk * SLC_BLOCK
    overlap = ((starts[:, None] < blk_start[None, :] + SLC_BLOCK) & (starts[:, None] + CMP_BLOCK > blk_start[None, :])).astype(jnp.float32)
    imp = jnp.einsum('btkgj,jn->btkn', p_cmp, overlap)
    cur = pos // SLC_BLOCK
    forced = (blk[None, :] == 0) | (blk[None, :] == cur[:, None]) | (blk[None, :] == cur[:, None] - 1)
    valid = blk[None, :] <= cur[:, None]
    score = jnp.where(forced[None, :, None, :], FORCE, imp)
    score = jnp.where(valid[None, :, None, :], score, -FORCE)
    _, sel = lax.top_k(score, min(TOP_K, n_slc))
    sel_ok = sel <= cur[None, :, None, None]
    n_past_blk = past_len // SLC_BLOCK
    per_page = PAGE_SIZE // SLC_BLOCK
    selc = jnp.minimum(sel, n_past_blk - 1)
    page = page_table[jnp.arange(B_)[:, None, None, None], selc // per_page]
    half = selc % per_page
    hi = jnp.arange(KV_HEADS)[None, None, :, None]

    def blocks(pool, new):
        p5 = pool.reshape(pool.shape[0], per_page, SLC_BLOCK, KV_HEADS, HEAD_DIM)
        g = p5[page, half, :, hi, :]
        newblk = jnp.pad(new.transpose(0, 2, 1, 3), ((0, 0), (0, 0), (0, SLC_BLOCK - T), (0, 0)))[:, None, :, None]
        return jnp.where((sel == n_past_blk)[..., None, None], newblk, g)

    o_slc = _select_attend(qr, sel, sel_ok, pos, blocks(past['pool_ks'], ks_new), blocks(past['pool_vs'], vs_new))
    return o_cmp, o_slc


def _nsa_mixer(z_q, z_kc, z_vc, z_ks, z_vs, z_kw, z_vw, z_g, pos, past, lw):
    B_, T = z_q.shape[:2]
    q = z_q.reshape(B_, T, N_HEADS, HEAD_DIM)
    qn = _rmsnorm(q, lw['nsa_q_norm'])
    qr = _partial_rope(qn, pos).reshape(B_, T, KV_HEADS, GQ, HEAD_DIM)
    qn = qn.reshape(B_, T, KV_HEADS, GQ, HEAD_DIM)
    kvshape = (B_, T, KV_HEADS, HEAD_DIM)
    kc_new = z_kc.reshape(kvshape)
    vc_new = z_vc.reshape(kvshape)
    ks_new = _partial_rope(_rmsnorm(z_ks.reshape(kvshape), lw['nsa_k_norm'][1]), pos)
    vs_new = z_vs.reshape(kvshape)
    kw_new = _partial_rope(_rmsnorm(z_kw.reshape(kvshape), lw['nsa_k_norm'][2]), pos)
    vw_new = z_vw.reshape(kvshape)
    wb = past['win_k'].shape[1]
    kw_all = jnp.concatenate([past['win_k'], kw_new], axis=1)
    vw_all = jnp.concatenate([past['win_v'], vw_new], axis=1)
    kpos = pos[0] - wb + jnp.arange(wb + T, dtype=jnp.int32)
    diff = pos[:, None] - kpos[None, :]
    mask = (diff >= 0) & (diff <= WINDOW)
    o_win, _ = _gqa_attend(qr, kw_all, vw_all, mask[None, :, None, None, :])
    win_k_new = kw_all[:, -wb:]
    win_v_new = vw_all[:, -wb:]
    o_cmp, o_slc = _nsa_compress_select(qn, qr, ks_new, vs_new, pos, past, lw)
    g = jax.nn.sigmoid(z_g.astype(jnp.float32)).reshape(B_, T, KV_HEADS, GQ, 3).astype(z_q.dtype)
    o = g[..., 0:1] * o_cmp + g[..., 1:2] * o_slc + g[..., 2:3] * o_win
    return o.reshape(B_, T, D_C), (kc_new, vc_new, ks_new, vs_new, win_k_new, win_v_new)


def _ffn(x2, g, wg, wu, wd):
    hn = rmsnorm_rows(x2, g)
    act = swiglu_up(hn, wg, wu)
    return mm_residual(act, wd, x2, 0.5)


def _pad_rows(x2, mult=16):
    m = x2.shape[0]
    mp = -(-m // mult) * mult
    return x2 if mp == m else jnp.pad(x2, ((0, mp - m), (0, 0)))


def _prompt_mixers(z3, zg3, pos, lw, wb):
    B_, T, _ = z3.shape
    y_a, tail = conv_prompt(z3, jnp.zeros((B_, CONV_HIST, D_A), jnp.float32), lw['conv_dw'], lw['conv_dw_bias'],
                            lw['conv_ln_g'], lw['conv_ln_b'])
    conv_new = tail[:, CONV_HIST - (CONV_WIDTH - 1):]
    y_b, s_last = s5_prompt(z3, jnp.zeros((B_, 1, 2 * N_STATE), jnp.float32), _s5_params(lw), lw['ssm_d'],
                            wb['ssm_w_glu'], lw['ssm_b_glu'])
    s_re = s_last[:, 0, :N_STATE].reshape(B_, SSM_GROUPS, SSM_STATE)
    s_im = s_last[:, 0, N_STATE:].reshape(B_, SSM_GROUPS, SSM_STATE)
    cosf, sinf = _rope_tables(pos)
    qn, qr, kc, vc, ks, vs, kw, vw, bks, bvs, bkw, bvw = nsa_prep(z3, cosf, sinf, lw['nsa_q_norm'], lw['nsa_k_norm'])
    kcmp, vcmp = nsa_compress(kc, vc, lw['nsa_cmp_wk'], lw['nsa_cmp_wv'], lw['nsa_k_norm'])
    y_c = nsa_attention(qn, qr, kcmp, vcmp, bks, bvs, bkw, bvw, zg3)
    kvshape = (B_, T, KV_HEADS, HEAD_DIM)
    keep = min(WINDOW, T)
    kv_new = (kc.reshape(kvshape), vc.reshape(kvshape), ks.reshape(kvshape), vs.reshape(kvshape),
              kw[:, -keep:].reshape(B_, keep, KV_HEADS, HEAD_DIM), vw[:, -keep:].reshape(B_, keep, KV_HEADS, HEAD_DIM))
    return y_a, y_b, y_c, kv_new + (conv_new, s_re, s_im)


def _sample_mixers(z2, z_g, pos, past, lw, wb, B_, T):
    assert T == 1
    m = B_ * T
    extra = z2.shape[0] - m
    hist = jnp.pad(past['conv'].transpose(1, 0, 2), ((0, 0), (0, extra), (0, 0)))
    y_a, hist_new = conv_step(z2[:, :D_A], z2[:, D_A:2 * D_A], hist, lw['conv_dw'], lw['conv_dw_bias'], lw['conv_ln_g'], lw['conv_ln_b'])
    conv_new = hist_new[:, :m].transpose(1, 0, 2)
    pad_state = lambda s: jnp.pad(s.reshape(m, N_STATE), ((0, extra), (0, 0)))
    y_b, s_re, s_im = s5_step(z2[:, 2 * D_A:2 * D_A + D_B], pad_state(past['s_re']), pad_state(past['s_im']), _s5_params(lw),
                              lw['ssm_d'], wb['ssm_w_glu'], lw['ssm_b_glu'])
    s_re = s_re[:m].reshape(B_, SSM_GROUPS, SSM_STATE)
    s_im = s_im[:m].reshape(B_, SSM_GROUPS, SSM_STATE)
    z = z2[:m].reshape(B_, T, D_MAIN)
    offs = []
    acc = Q_COL0
    for n in (D_C, KVD, KVD, KVD, KVD, KVD):
        acc += n
        offs.append(acc)
    z_q, z_kc, z_vc, z_ks, z_vs, z_kw, z_vw = jnp.split(z[..., Q_COL0:], [o - Q_COL0 for o in offs], axis=-1)
    y_c, kv_new = _nsa_mixer(z_q, z_kc, z_vc, z_ks, z_vs, z_kw, z_vw, z_g, pos, past, lw)
    return y_a[:m], y_b[:m], y_c, tuple(kv_new) + (conv_new, s_re, s_im)


def _layer_forward(x, pos, past, lw, wb):
    B_, T, _ = x.shape
    m = B_ * T
    x2 = _pad_rows(x.reshape(m, D_MODEL))
    h2 = _ffn(x2, lw['ffn1_norm'], wb['ffn1_w_gate'], wb['ffn1_w_up'], wb['ffn1_w_down'])
    u2 = rmsnorm_rows(h2, lw['mix_norm'])
    z = mm_bias(u2, wb['w_in_main'], jnp.zeros((D_MAIN,), jnp.float32))
    zg = mm_bias(u2, wb['w_in_gate'], jnp.zeros((KV_HEADS * LANE,), jnp.float32))
    if past is None:
        y_a, y_b, y_c, new_state = _prompt_mixers(z.reshape(B_, T, D_MAIN), zg.reshape(B_, T, KV_HEADS * LANE), pos, lw, wb)
    else:
        z_g = zg[:m].reshape(B_, T, KV_HEADS, LANE)[..., :3 * GQ].reshape(B_, T, N_GATE)
        y_a, y_b, y_c, new_state = _sample_mixers(z, z_g, pos, past, lw, wb, B_, T)
    rows = lambda y: _pad_rows(y.reshape(m, -1)).astype(jnp.bfloat16)
    mixed = gated_merge(u2, rows(y_a), rows(y_b), rows(y_c), wb['merge_w_gate'], lw['merge_b_gate'],
                        wb['conv_w_out'], wb['ssm_w_out'], wb['nsa_w_out'])
    h2 = mm_residual(mixed, wb['w_out'], h2, 1.0)
    y2 = _ffn(h2, lw['ffn2_norm'], wb['ffn2_w_gate'], wb['ffn2_w_up'], wb['ffn2_w_down'])
    return y2[:m].reshape(B_, T, D_MODEL), new_state


def kernel(x_prompt, x_sample, cache_cmp_k, cache_cmp_v, cache_slc_k, cache_slc_v, cache_win_k, cache_win_v, state_conv, state_ssm_re, state_ssm_im, page_table, ffn1_norm, ffn1_w_gate, ffn1_w_up, ffn1_w_down, mix_norm, w_in, conv_dw, conv_dw_bias, conv_ln_g, conv_ln_b, conv_w_out, ssm_lambda_re, ssm_lambda_im, ssm_log_dt, ssm_b_re, ssm_b_im, ssm_c_re, ssm_c_im, ssm_d, ssm_w_glu, ssm_b_glu, ssm_w_out, nsa_q_norm, nsa_k_norm, nsa_cmp_wk, nsa_cmp_wv, nsa_w_out, merge_w_gate, merge_b_gate, w_out, ffn2_norm, ffn2_w_gate, ffn2_w_up, ffn2_w_down):
    past_len = page_table.shape[1] * PAGE_SIZE
    pos_p = jnp.arange(x_prompt.shape[1], dtype=jnp.int32)
    pos_s = past_len + jnp.arange(x_sample.shape[1], dtype=jnp.int32)
    hp, hs = x_prompt, x_sample
    st_p, st_s = [], []
    bf = jnp.bfloat16
    for l in range(DEPTH):
        lw = dict(ffn1_norm=ffn1_norm[l], mix_norm=mix_norm[l],
                  conv_dw=conv_dw[l], conv_dw_bias=conv_dw_bias[l], conv_ln_g=conv_ln_g[l], conv_ln_b=conv_ln_b[l],
                  ssm_lambda_re=ssm_lambda_re[l], ssm_lambda_im=ssm_lambda_im[l], ssm_log_dt=ssm_log_dt[l],
                  ssm_b_re=ssm_b_re[l], ssm_b_im=ssm_b_im[l], ssm_c_re=ssm_c_re[l], ssm_c_im=ssm_c_im[l],
                  ssm_d=ssm_d[l], ssm_b_glu=ssm_b_glu[l],
                  nsa_q_norm=nsa_q_norm[l], nsa_k_norm=nsa_k_norm[l], nsa_cmp_wk=nsa_cmp_wk[l], nsa_cmp_wv=nsa_cmp_wv[l],
                  merge_b_gate=merge_b_gate[l], ffn2_norm=ffn2_norm[l])
        wb = dict(ffn1_w_gate=ffn1_w_gate[l].astype(bf), ffn1_w_up=ffn1_w_up[l].astype(bf), ffn1_w_down=ffn1_w_down[l].astype(bf),
                  w_in_main=w_in[l][:, :D_MAIN].astype(bf),
                  w_in_gate=jnp.pad(w_in[l][:, D_MAIN:].reshape(D_MODEL, KV_HEADS, 3 * GQ), ((0, 0), (0, 0), (0, LANE - 3 * GQ))).reshape(D_MODEL, KV_HEADS * LANE).astype(bf),
                  conv_w_out=conv_w_out[l].astype(bf), ssm_w_glu=ssm_w_glu[l].astype(bf), ssm_w_out=ssm_w_out[l].astype(bf),
                  nsa_w_out=nsa_w_out[l].astype(bf), merge_w_gate=merge_w_gate[l].astype(bf), w_out=w_out[l].astype(bf),
                  ffn2_w_gate=ffn2_w_gate[l].astype(bf), ffn2_w_up=ffn2_w_up[l].astype(bf), ffn2_w_down=ffn2_w_down[l].astype(bf))
        hp, new_p = _layer_forward(hp, pos_p, None, lw, wb)
        past = dict(pool_kc=cache_cmp_k[l], pool_vc=cache_cmp_v[l], pool_ks=cache_slc_k[l], pool_vs=cache_slc_v[l],
                    page_table=page_table, win_k=cache_win_k[l], win_v=cache_win_v[l], conv=state_conv[l],
                    s_re=state_ssm_re[l], s_im=state_ssm_im[l])
        hs, new_s = _layer_forward(hs, pos_s, past, lw, wb)
        st_p.append(new_p)
        st_s.append(new_s)
    P = [jnp.stack([st[i] for st in st_p], axis=0) for i in range(9)]
    S = [jnp.stack([st[i] for st in st_s], axis=0) for i in range(9)]
    return (hp, hs, P[0], P[1], P[2], P[3], P[4], P[5], P[6], P[7], P[8], S[0], S[1], S[2], S[3], S[4], S[5], S[6], S[7], S[8])
```

```python
import functools
import math

import jax
import jax.numpy as jnp
from jax import lax
from jax.experimental import pallas as pl
from jax.experimental.pallas import tpu as pltpu

D_MODEL = 4096
DEPTH = 4
PAGE_SIZE = 128
D_A = D_MODEL // 4
D_B = D_MODEL // 4
D_C = D_MODEL // 2
HEAD_DIM = 128
N_HEADS = D_C // HEAD_DIM
KV_HEADS = 4
GQ = N_HEADS // KV_HEADS
KVD = KV_HEADS * HEAD_DIM
CONV_WIDTH = 31
SSM_GROUP = 16
SSM_GROUPS = D_B // SSM_GROUP
SSM_STATE = 64
CMP_BLOCK = 32
CMP_STRIDE = 16
SLC_BLOCK = 64
TOP_K = 16
WINDOW = 512
WIN_QB = 128
SLC_Q_CHUNK = 32
ROPE_DIM = HEAD_DIM // 4
ROPE_THETA = 500000.0
D_FF = ((8 * D_MODEL // 3 + 255) // 256) * 256
EPS = 1e-6
SCALE = HEAD_DIM ** -0.5
NEG = -1e30
FORCE = 1e4
D_MAIN = 2 * D_A + D_B + D_C + 6 * KVD
N_GATE = 3 * N_HEADS
LANE = 128

VMEM_LIMIT_BYTES = 56 * 1024 * 1024


def _cparams(*sem):
    return pltpu.CompilerParams(dimension_semantics=sem, vmem_limit_bytes=VMEM_LIMIT_BYTES)


def _rmsnorm_kernel(x_ref, g_ref, o_ref):
    x = x_ref[...]
    ms = jnp.mean(x * x, axis=-1, keepdims=True)
    o_ref[...] = (x * lax.rsqrt(ms + EPS) * g_ref[...]).astype(o_ref.dtype)


def rmsnorm_rows(x, g, out_dtype=jnp.bfloat16):
    m, d = x.shape
    tr = min(m, 256)
    return pl.pallas_call(
        _rmsnorm_kernel,
        grid=(m // tr,),
        in_specs=[pl.BlockSpec((tr, d), lambda i: (i, 0)), pl.BlockSpec((1, d), lambda i: (0, 0))],
        out_specs=pl.BlockSpec((tr, d), lambda i: (i, 0)),
        out_shape=jax.ShapeDtypeStruct((m, d), out_dtype),
        compiler_params=_cparams("parallel"),
        name="rmsnorm_rows",
    )(x, g.reshape(1, d))


def _swiglu_up_kernel(a_ref, wg_ref, wu_ref, o_ref):
    a = a_ref[...]
    hg = jnp.dot(a, wg_ref[...], preferred_element_type=jnp.float32)
    hu = jnp.dot(a, wu_ref[...], preferred_element_type=jnp.float32)
    o_ref[...] = (hg * jax.nn.sigmoid(hg) * hu).astype(o_ref.dtype)


def swiglu_up(a, wg, wu):
    m, k = a.shape
    f = wg.shape[1]
    tm = min(m, 2048)
    tn = 256
    return pl.pallas_call(
        _swiglu_up_kernel,
        grid=(m // tm, f // tn),
        in_specs=[pl.BlockSpec((tm, k), lambda i, j: (i, 0)),
                  pl.BlockSpec((k, tn), lambda i, j: (0, j)),
                  pl.BlockSpec((k, tn), lambda i, j: (0, j))],
        out_specs=pl.BlockSpec((tm, tn), lambda i, j: (i, j)),
        out_shape=jax.ShapeDtypeStruct((m, f), jnp.bfloat16),
        compiler_params=_cparams("parallel", "arbitrary"),
        name="swiglu_up",
    )(a, wg, wu)


def _mm_res_kernel(a_ref, w_ref, r_ref, o_ref, acc_ref, *, nk, scale):
    kk = pl.program_id(2)
    part = jnp.dot(a_ref[...], w_ref[...], preferred_element_type=jnp.float32)

    @pl.when(kk == 0)
    def _():
        acc_ref[...] = part

    @pl.when(kk > 0)
    def _():
        acc_ref[...] += part

    @pl.when(kk == nk - 1)
    def _():
        o_ref[...] = r_ref[...] + scale * acc_ref[...]


def mm_residual(a, w, res, scale):
    m, k = a.shape
    n = w.shape[1]
    tm = min(m, 1024)
    tn = 512
    tk = k if k <= 4096 else k // 2
    nk = k // tk
    return pl.pallas_call(
        functools.partial(_mm_res_kernel, nk=nk, scale=scale),
        grid=(m // tm, n // tn, nk),
        in_specs=[pl.BlockSpec((tm, tk), lambda i, j, kk: (i, kk)),
                  pl.BlockSpec((tk, tn), lambda i, j, kk: (kk, j)),
                  pl.BlockSpec((tm, tn), lambda i, j, kk: (i, j))],
        out_specs=pl.BlockSpec((tm, tn), lambda i, j, kk: (i, j)),
        out_shape=jax.ShapeDtypeStruct((m, n), jnp.float32),
        scratch_shapes=[pltpu.VMEM((tm, tn), jnp.float32)],
        compiler_params=_cparams("parallel", "arbitrary", "arbitrary"),
        name="mm_residual",
    )(a, w, res)


def _mm_bias_kernel(a_ref, w_ref, b_ref, o_ref, *, act):
    y = jnp.dot(a_ref[...], w_ref[...], preferred_element_type=jnp.float32) + b_ref[...]
    if act == "sigmoid":
        y = jax.nn.sigmoid(y)
    o_ref[...] = y.astype(o_ref.dtype)


def mm_bias(a, w, bias, act=None, out_dtype=jnp.float32):
    m, k = a.shape
    n = w.shape[1]
    tm = min(m, 1024)
    tn = 512 if n % 512 == 0 else n
    return pl.pallas_call(
        functools.partial(_mm_bias_kernel, act=act),
        grid=(m // tm, n // tn),
        in_specs=[pl.BlockSpec((tm, k), lambda i, j: (i, 0)),
                  pl.BlockSpec((k, tn), lambda i, j: (0, j)),
                  pl.BlockSpec((1, tn), lambda i, j: (0, j))],
        out_specs=pl.BlockSpec((tm, tn), lambda i, j: (i, j)),
        out_shape=jax.ShapeDtypeStruct((m, n), out_dtype),
        compiler_params=_cparams("parallel", "arbitrary"),
        name="mm_bias",
    )(a, w, bias.reshape(1, n))


def _merge_kernel(u_ref, ya_ref, yb_ref, yc_ref, wga_ref, wgb_ref, wgc_ref, bga_ref, bgb_ref, bgc_ref,
                  wa_ref, wb_ref, wc_ref, o_ref):
    u = u_ref[...]
    acc = None
    for y_ref, w_ref, wg_ref, bg_ref in ((ya_ref, wa_ref, wga_ref, bga_ref), (yb_ref, wb_ref, wgb_ref, bgb_ref),
                                         (yc_ref, wc_ref, wgc_ref, bgc_ref)):
        gate = jax.nn.sigmoid(jnp.dot(u, wg_ref[...], preferred_element_type=jnp.float32) + bg_ref[...])
        term = gate * jnp.dot(y_ref[...], w_ref[...], preferred_element_type=jnp.float32)
        acc = term if acc is None else acc + term
    o_ref[...] = acc.astype(o_ref.dtype)


def gated_merge(u, ya, yb, yc, w_gate, b_gate, wa, wb, wc):
    m, d = u.shape
    tm = min(m, 512)
    tn = 256
    nj = d // tn
    row = lambda width: pl.BlockSpec((tm, width), lambda i, j: (i, 0))

    def gcol(rows, br):
        return pl.BlockSpec((rows, tn), lambda i, j: (0, j + br * nj))

    col = lambda rows: pl.BlockSpec((rows, tn), lambda i, j: (0, j))
    return pl.pallas_call(
        _merge_kernel,
        grid=(m // tm, nj),
        in_specs=[row(d), row(ya.shape[1]), row(yb.shape[1]), row(yc.shape[1]),
                  gcol(d, 0), gcol(d, 1), gcol(d, 2), gcol(1, 0), gcol(1, 1), gcol(1, 2),
                  col(ya.shape[1]), col(yb.shape[1]), col(yc.shape[1])],
        out_specs=pl.BlockSpec((tm, tn), lambda i, j: (i, j)),
        out_shape=jax.ShapeDtypeStruct((m, d), jnp.bfloat16),
        compiler_params=_cparams("parallel", "arbitrary"),
        name="gated_merge",
    )(u, ya, yb, yc, w_gate, w_gate, w_gate, b_gate.reshape(1, 3 * d), b_gate.reshape(1, 3 * d), b_gate.reshape(1, 3 * d),
      wa, wb, wc)


N_STATE = SSM_GROUPS * SSM_STATE
S5_LANE_GROUPS = D_B // LANE
S5_SLAB = N_STATE // S5_LANE_GROUPS
S5_ROWS = 256


def _s5_kernel(z_ref, s0_ref, lr_ref, li_ref, wre_ref, wim_ref, cr_ref, ci_ref, d_ref, wglu_ref, bglu_ref,
               y_ref, slast_ref, bre_ref, bim_ref, sre_ref, sim_ref, yacc_ref):
    tt = z_ref.shape[0]

    @pl.when(pl.program_id(1) == 0)
    def _():
        sre_ref[...] = s0_ref[:, :N_STATE]
        sim_ref[...] = s0_ref[:, N_STATE:]

    x = z_ref[...]
    xb = x.astype(jnp.bfloat16)
    for j in range(S5_LANE_GROUPS):
        xj = xb[:, j * LANE:(j + 1) * LANE]
        cs = slice(j * S5_SLAB, (j + 1) * S5_SLAB)
        bre_ref[:, cs] = jnp.dot(xj, wre_ref[j], preferred_element_type=jnp.float32)
        bim_ref[:, cs] = jnp.dot(xj, wim_ref[j], preferred_element_type=jnp.float32)

    for c in range(S5_LANE_GROUPS):
        cs = slice(c * S5_SLAB, (c + 1) * S5_SLAB)
        lr = lr_ref[:, cs]
        li = li_ref[:, cs]

        def step(t, carry, cs=cs, lr=lr, li=li):
            sr, si = carry
            nr = lr * sr - li * si + bre_ref[pl.ds(t, 1), cs]
            ni = lr * si + li * sr + bim_ref[pl.ds(t, 1), cs]
            bre_ref[pl.ds(t, 1), cs] = nr
            bim_ref[pl.ds(t, 1), cs] = ni
            return nr, ni

        sr, si = lax.fori_loop(0, tt, step, (sre_ref[:, cs], sim_ref[:, cs]), unroll=8)
        sre_ref[:, cs] = sr
        sim_ref[:, cs] = si

    for j in range(S5_LANE_GROUPS):
        cs = slice(j * S5_SLAB, (j + 1) * S5_SLAB)
        yj = jnp.dot(bre_ref[:, cs].astype(jnp.bfloat16), cr_ref[j], preferred_element_type=jnp.float32)
        yj = yj + jnp.dot(bim_ref[:, cs].astype(jnp.bfloat16), ci_ref[j], preferred_element_type=jnp.float32)
        yacc_ref[:, j * LANE:(j + 1) * LANE] = yj
    y = jax.nn.gelu(yacc_ref[...] + d_ref[...] * x)
    gl = jnp.dot(y.astype(jnp.bfloat16), wglu_ref[...], preferred_element_type=jnp.float32) + bglu_ref[...]
    y_ref[...] = (y * jax.nn.sigmoid(gl)).astype(y_ref.dtype)
    slast_ref[:, :N_STATE] = sre_ref[...]
    slast_ref[:, N_STATE:] = sim_ref[...]


def _s5_params(lw):
    lam = lax.complex(lw['ssm_lambda_re'], lw['ssm_lambda_im'])
    dt = jnp.exp(lw['ssm_log_dt'])[:, None]
    lam_bar = jnp.exp(lam * dt)
    b_bar = ((lam_bar - 1.0) / lam)[..., None] * lax.complex(lw['ssm_b_re'], lw['ssm_b_im'])
    gpl = LANE // SSM_GROUP
    eye = jnp.eye(gpl, dtype=jnp.float32)

    def bdiag_in(b):
        b4 = b.reshape(S5_LANE_GROUPS, gpl, SSM_STATE, SSM_GROUP)
        m = jnp.einsum('jgpc,gh->jgchp', b4, eye)
        return m.reshape(S5_LANE_GROUPS, LANE, S5_SLAB).astype(jnp.bfloat16)

    def bdiag_out(cm):
        c4 = cm.reshape(S5_LANE_GROUPS, gpl, SSM_GROUP, SSM_STATE)
        m = jnp.einsum('jgcp,gh->jgphc', c4, eye)
        return m.reshape(S5_LANE_GROUPS, S5_SLAB, LANE).astype(jnp.bfloat16)

    return dict(lr=jnp.real(lam_bar).reshape(1, N_STATE), li=jnp.imag(lam_bar).reshape(1, N_STATE),
                wre=bdiag_in(jnp.real(b_bar)), wim=bdiag_in(jnp.imag(b_bar)),
                cr=bdiag_out(lw['ssm_c_re']), ci=bdiag_out(-lw['ssm_c_im']))


def s5_prompt(z3, s0, sp, d, wglu, bglu):
    b, t, _ = z3.shape
    tt = min(t, S5_ROWS)
    zcol = (2 * D_A) // D_B
    const3 = lambda bi, ti: (0, 0, 0)
    const2 = lambda bi, ti: (0, 0)
    return pl.pallas_call(
        _s5_kernel,
        grid=(b, t // tt),
        in_specs=[pl.BlockSpec((None, tt, D_B), lambda bi, ti: (bi, ti, zcol)),
                  pl.BlockSpec((None, 1, 2 * N_STATE), lambda bi, ti: (bi, 0, 0)),
                  pl.BlockSpec((1, N_STATE), const2), pl.BlockSpec((1, N_STATE), const2),
                  pl.BlockSpec((S5_LANE_GROUPS, LANE, S5_SLAB), const3), pl.BlockSpec((S5_LANE_GROUPS, LANE, S5_SLAB), const3),
                  pl.BlockSpec((S5_LANE_GROUPS, S5_SLAB, LANE), const3), pl.BlockSpec((S5_LANE_GROUPS, S5_SLAB, LANE), const3),
                  pl.BlockSpec((1, D_B), const2), pl.BlockSpec((D_B, D_B), const2), pl.BlockSpec((1, D_B), const2)],
        out_specs=[pl.BlockSpec((None, tt, D_B), lambda bi, ti: (bi, ti, 0)),
                   pl.BlockSpec((None, 1, 2 * N_STATE), lambda bi, ti: (bi, 0, 0))],
        out_shape=[jax.ShapeDtypeStruct((b, t, D_B), jnp.bfloat16),
                   jax.ShapeDtypeStruct((b, 1, 2 * N_STATE), jnp.float32)],
        scratch_shapes=[pltpu.VMEM((tt, N_STATE), jnp.float32), pltpu.VMEM((tt, N_STATE), jnp.float32),
                        pltpu.VMEM((1, N_STATE), jnp.float32), pltpu.VMEM((1, N_STATE), jnp.float32),
                        pltpu.VMEM((tt, D_B), jnp.float32)],
        compiler_params=_cparams("parallel", "arbitrary"),
        name="s5_prompt",
    )(z3, s0, sp['lr'], sp['li'], sp['wre'], sp['wim'], sp['cr'], sp['ci'], d.reshape(1, D_B), wglu, bglu.reshape(1, D_B))


def _s5_step_kernel(x_ref, sre_ref, sim_ref, lr_ref, li_ref, wre_ref, wim_ref, cr_ref, ci_ref, d_ref, wglu_ref, bglu_ref,
                    y_ref, nre_ref, nim_ref):
    x = x_ref[...]
    xb = x.astype(jnp.bfloat16)
    cols = []
    for j in range(S5_LANE_GROUPS):
        xj = xb[:, j * LANE:(j + 1) * LANE]
        cs = slice(j * S5_SLAB, (j + 1) * S5_SLAB)
        sr, si, lr, li = sre_ref[:, cs], sim_ref[:, cs], lr_ref[:, cs], li_ref[:, cs]
        nr = lr * sr - li * si + jnp.dot(xj, wre_ref[j], preferred_element_type=jnp.float32)
        ni = lr * si + li * sr + jnp.dot(xj, wim_ref[j], preferred_element_type=jnp.float32)
        nre_ref[:, cs] = nr
        nim_ref[:, cs] = ni
        cols.append(jnp.dot(nr.astype(jnp.bfloat16), cr_ref[j], preferred_element_type=jnp.float32)
                    + jnp.dot(ni.astype(jnp.bfloat16), ci_ref[j], preferred_element_type=jnp.float32))
    y = jax.nn.gelu(jnp.concatenate(cols, axis=1) + d_ref[...] * x)
    gl = jnp.dot(y.astype(jnp.bfloat16), wglu_ref[...], preferred_element_type=jnp.float32) + bglu_ref[...]
    y_ref[...] = (y * jax.nn.sigmoid(gl)).astype(y_ref.dtype)


def s5_step(x, s_re, s_im, sp, d, wglu, bglu):
    r = x.shape[0]
    st = jax.ShapeDtypeStruct((r, N_STATE), jnp.float32)
    return pl.pallas_call(
        _s5_step_kernel,
        out_shape=[jax.ShapeDtypeStruct((r, D_B), jnp.bfloat16), st, st],
        compiler_params=pltpu.CompilerParams(vmem_limit_bytes=VMEM_LIMIT_BYTES),
        name="s5_step",
    )(x, s_re, s_im, sp['lr'], sp['li'], sp['wre'], sp['wim'], sp['cr'], sp['ci'], d.reshape(1, D_B), wglu, bglu.reshape(1, D_B))


def _conv_step_kernel(zl_ref, zg_ref, st_ref, dw_ref, dwb_ref, lng_ref, lnb_ref, y_ref, new_ref):
    a = zl_ref[...] * jax.nn.sigmoid(zg_ref[...])
    hist = CONV_WIDTH - 1
    acc = a * dw_ref[hist:hist + 1, :] + dwb_ref[...]
    for k in range(hist):
        acc = acc + st_ref[k] * dw_ref[k:k + 1, :]
    mu = jnp.mean(acc, axis=-1, keepdims=True)
    yc = acc - mu
    var = jnp.mean(yc * yc, axis=-1, keepdims=True)
    yn = yc * lax.rsqrt(var + EPS) * lng_ref[...] + lnb_ref[...]
    y_ref[...] = (yn * jax.nn.sigmoid(yn)).astype(y_ref.dtype)
    new_ref[0:hist - 1] = st_ref[1:hist]
    new_ref[hist - 1] = a


def conv_step(zl, zg, hist, dw, dwb, lng, lnb):
    r = zl.shape[0]
    return pl.pallas_call(
        _conv_step_kernel,
        out_shape=[jax.ShapeDtypeStruct((r, D_A), jnp.bfloat16), jax.ShapeDtypeStruct(hist.shape, jnp.float32)],
        compiler_params=pltpu.CompilerParams(vmem_limit_bytes=VMEM_LIMIT_BYTES),
        name="conv_step",
    )(zl, zg, hist, dw, dwb.reshape(1, D_A), lng.reshape(1, D_A), lnb.reshape(1, D_A))


CONV_HIST = 32
CONV_ROWS = 256


def _conv_kernel(zl_ref, zg_ref, buf_ref, dw_ref, dwb_ref, lng_ref, lnb_ref, y_ref, tail_ref, xpad_ref):
    tt = zl_ref.shape[0]

    @pl.when(pl.program_id(1) == 0)
    def _():
        xpad_ref[0:CONV_HIST, :] = buf_ref[...]

    xpad_ref[CONV_HIST:CONV_HIST + tt, :] = zl_ref[...] * jax.nn.sigmoid(zg_ref[...])
    off = CONV_HIST - (CONV_WIDTH - 1)
    acc = xpad_ref[off:off + tt, :] * dw_ref[0:1, :]
    for k in range(1, CONV_WIDTH):
        acc = acc + xpad_ref[off + k:off + k + tt, :] * dw_ref[k:k + 1, :]
    y = acc + dwb_ref[...]
    mu = jnp.mean(y, axis=-1, keepdims=True)
    yc = y - mu
    var = jnp.mean(yc * yc, axis=-1, keepdims=True)
    yn = yc * lax.rsqrt(var + EPS) * lng_ref[...] + lnb_ref[...]
    y_ref[...] = (yn * jax.nn.sigmoid(yn)).astype(y_ref.dtype)
    tail = xpad_ref[tt:tt + CONV_HIST, :]
    tail_ref[...] = tail
    xpad_ref[0:CONV_HIST, :] = tail


def conv_prompt(z3, buf, dw, dwb, lng, lnb):
    b, t, _ = z3.shape
    tt = min(t, CONV_ROWS)
    const2 = lambda bi, ti: (0, 0)
    return pl.pallas_call(
        _conv_kernel,
        grid=(b, t // tt),
        in_specs=[pl.BlockSpec((None, tt, D_A), lambda bi, ti: (bi, ti, 0)),
                  pl.BlockSpec((None, tt, D_A), lambda bi, ti: (bi, ti, 1)),
                  pl.BlockSpec((None, CONV_HIST, D_A), lambda bi, ti: (bi, 0, 0)),
                  pl.BlockSpec((CONV_HIST, D_A), const2), pl.BlockSpec((1, D_A), const2),
                  pl.BlockSpec((1, D_A), const2), pl.BlockSpec((1, D_A), const2)],
        out_specs=[pl.BlockSpec((None, tt, D_A), lambda bi, ti: (bi, ti, 0)),
                   pl.BlockSpec((None, CONV_HIST, D_A), lambda bi, ti: (bi, 0, 0))],
        out_shape=[jax.ShapeDtypeStruct((b, t, D_A), jnp.bfloat16),
                   jax.ShapeDtypeStruct((b, CONV_HIST, D_A), jnp.float32)],
        scratch_shapes=[pltpu.VMEM((tt + CONV_HIST, D_A), jnp.float32)],
        compiler_params=_cparams("parallel", "arbitrary"),
        name="conv_prompt",
    )(z3, z3, buf, jnp.pad(dw, ((0, CONV_HIST - CONV_WIDTH), (0, 0))), dwb.reshape(1, D_A), lng.reshape(1, D_A), lnb.reshape(1, D_A))


NSA_ROWS = 256
SLC_KEY_EXT = 512
Q_COL0 = 2 * D_A + D_B
KV_COL0 = Q_COL0 + D_C
CMP_PER_ROW = 16 * KVD


def _head_rmsnorm(x, g):
    ms = jnp.mean(x * x, axis=-1, keepdims=True)
    return x * lax.rsqrt(ms + EPS) * g


def _rope(x, cosf, sinf, lane):
    half = ROPE_DIM // 2
    rot = jnp.where(lane < half, pltpu.roll(x, HEAD_DIM - half, 1), pltpu.roll(x, half, 1))
    return x * cosf + rot * sinf


def _nsa_prep_kernel(zq0_ref, zq1_ref, kc_ref, vc_ref, ks_ref, vs_ref, kw_ref, vw_ref, cos_ref, sin_ref, qg_ref, kg_ref,
                     qn_ref, qr_ref, okc_ref, ovc_ref, oks_ref, ovs_ref, okw_ref, ovw_ref,
                     bks_ref, bvs_ref, bkw_ref, bvw_ref):
    tt = cos_ref.shape[0]
    cosf = cos_ref[...]
    sinf = sin_ref[...]
    lane = lax.broadcasted_iota(jnp.int32, (tt, HEAD_DIM), 1)
    qg = qg_ref[...]
    heads_per_ref = N_HEADS // 2
    for h in range(N_HEADS):
        src = zq0_ref if h < heads_per_ref else zq1_ref
        hh = h % heads_per_ref
        xn = _head_rmsnorm(src[:, hh * HEAD_DIM:(hh + 1) * HEAD_DIM], qg)
        qn_ref[:, h * HEAD_DIM:(h + 1) * HEAD_DIM] = xn.astype(qn_ref.dtype)
        qr_ref[:, h * HEAD_DIM:(h + 1) * HEAD_DIM] = _rope(xn, cosf, sinf, lane).astype(qr_ref.dtype)
    okc_ref[...] = kc_ref[...]
    ovc_ref[...] = vc_ref[...]
    vs = vs_ref[...]
    ovs_ref[...] = vs
    bvs_ref[...] = vs.astype(bvs_ref.dtype)
    vw = vw_ref[...]
    ovw_ref[...] = vw
    bvw_ref[...] = vw.astype(bvw_ref.dtype)
    for h in range(KV_HEADS):
        hs = slice(h * HEAD_DIM, (h + 1) * HEAD_DIM)
        ks = _rope(_head_rmsnorm(ks_ref[:, hs], kg_ref[1:2, :]), cosf, sinf, lane)
        oks_ref[:, hs] = ks
        bks_ref[:, hs] = ks.astype(bks_ref.dtype)
        kw = _rope(_head_rmsnorm(kw_ref[:, hs], kg_ref[2:3, :]), cosf, sinf, lane)
        okw_ref[:, hs] = kw
        bkw_ref[:, hs] = kw.astype(bkw_ref.dtype)


def nsa_prep(z3, cosf, sinf, q_norm, k_norm):
    b, t, _ = z3.shape
    tt = min(t, NSA_ROWS)
    half_q = D_C // 2
    kvb = KV_COL0 // KVD

    def zcol(width, idx):
        return pl.BlockSpec((None, tt, width), lambda bi, ti: (bi, ti, idx))

    row = lambda width: pl.BlockSpec((None, tt, width), lambda bi, ti: (bi, ti, 0))
    const2 = lambda bi, ti: (0, 0)
    f32, bf16 = jnp.float32, jnp.bfloat16
    kv_f32 = jax.ShapeDtypeStruct((b, t, KVD), f32)
    kv_bf = jax.ShapeDtypeStruct((b, t, KVD), bf16)
    return pl.pallas_call(
        _nsa_prep_kernel,
        grid=(b, t // tt),
        in_specs=[zcol(half_q, Q_COL0 // half_q), zcol(half_q, Q_COL0 // half_q + 1)]
                 + [zcol(KVD, kvb + i) for i in range(6)]
                 + [pl.BlockSpec((tt, HEAD_DIM), lambda bi, ti: (ti, 0)), pl.BlockSpec((tt, HEAD_DIM), lambda bi, ti: (ti, 0)),
                    pl.BlockSpec((1, HEAD_DIM), const2), pl.BlockSpec((3, HEAD_DIM), const2)],
        out_specs=[row(D_C), row(D_C)] + [row(KVD)] * 10,
        out_shape=[jax.ShapeDtypeStruct((b, t, D_C), bf16), jax.ShapeDtypeStruct((b, t, D_C), bf16)]
                  + [kv_f32] * 6 + [kv_bf] * 4,
        compiler_params=_cparams("parallel", "arbitrary"),
        name="nsa_prep",
    )(z3, z3, z3, z3, z3, z3, z3, z3, cosf, sinf, q_norm.reshape(1, HEAD_DIM), k_norm)


def _compress_kernel(xk_ref, xks_ref, xv_ref, xvs_ref, wk_ref, wv_ref, kg_ref, kc_ref, vc_ref):
    half = CMP_BLOCK // 2

    def pooled(x_ref, xs_ref, w_ref):
        acc = x_ref[:, 0:KVD] * w_ref[0:1, :] + xs_ref[:, 0:KVD] * w_ref[half:half + 1, :]
        for i in range(1, half):
            cs = slice(i * KVD, (i + 1) * KVD)
            acc = acc + x_ref[:, cs] * w_ref[i:i + 1, :] + xs_ref[:, cs] * w_ref[half + i:half + i + 1, :]
        return acc

    kc = pooled(xk_ref, xks_ref, wk_ref)
    vc = pooled(xv_ref, xvs_ref, wv_ref)
    vc_ref[...] = vc.astype(vc_ref.dtype)
    for h in range(KV_HEADS):
        hs = slice(h * HEAD_DIM, (h + 1) * HEAD_DIM)
        kc_ref[:, hs] = _head_rmsnorm(kc[:, hs], kg_ref[0:1, :]).astype(kc_ref.dtype)


def nsa_compress(kc_rows, vc_rows, cmp_wk, cmp_wv, k_norm):
    b, t, _ = kc_rows.shape
    nb = t // CMP_STRIDE
    rb = min(nb, 64)

    def views(x):
        x2 = x.reshape(b, nb, CMP_PER_ROW)
        return x2, jnp.concatenate([x2[:, 1:], jnp.zeros((b, 1, CMP_PER_ROW), x.dtype)], axis=1)

    xk, xks = views(kc_rows)
    xv, xvs = views(vc_rows)
    blk = pl.BlockSpec((None, rb, CMP_PER_ROW), lambda bi, ri: (bi, ri, 0))
    const2 = lambda bi, ri: (0, 0)
    out = pl.BlockSpec((None, rb, KVD), lambda bi, ri: (bi, ri, 0))
    return pl.pallas_call(
        _compress_kernel,
        grid=(b, nb // rb),
        in_specs=[blk, blk, blk, blk, pl.BlockSpec((CMP_BLOCK, KVD), const2), pl.BlockSpec((CMP_BLOCK, KVD), const2),
                  pl.BlockSpec((3, HEAD_DIM), const2)],
        out_specs=[out, out],
        out_shape=[jax.ShapeDtypeStruct((b, nb, KVD), jnp.bfloat16)] * 2,
        compiler_params=_cparams("parallel", "arbitrary"),
        name="nsa_compress",
    )(xk, xks, xv, xvs, cmp_wk.transpose(1, 0, 2).reshape(CMP_BLOCK, KVD), cmp_wv.transpose(1, 0, 2).reshape(CMP_BLOCK, KVD), k_norm)


def _dot_nt(a, b):
    return lax.dot_general(a, b, (((1,), (1,)), ((), ())), preferred_element_type=jnp.float32)


def _softmax_rows(s, mask):
    s = jnp.where(mask, s, NEG)
    m = jnp.max(s, axis=-1, keepdims=True)
    e = jnp.where(mask, jnp.exp(s - m), 0.0)
    return e / jnp.maximum(jnp.sum(e, axis=-1, keepdims=True), 1e-30)


def _nsa_attn_kernel(qn_ref, qr_ref, kc_ref, vc_ref, ks_ref, vs_ref, kw_ref, vw_ref, zg_ref, ov_ref, ex_ref, o_ref, oslc_ref):
    tq = qn_ref.shape[0]
    t_all = ks_ref.shape[0]
    n_cmp_pad = kc_ref.shape[0]
    t0 = pl.program_id(2) * tq
    f32, bf16 = jnp.float32, jnp.bfloat16

    qn = jnp.concatenate([qn_ref[:, g * HEAD_DIM:(g + 1) * HEAD_DIM] for g in range(GQ)], axis=0)
    s = _dot_nt(qn, kc_ref[...]) * SCALE
    rows = lax.broadcasted_iota(jnp.int32, (GQ * tq, n_cmp_pad), 0)
    tpos_c = t0 + (rows & (tq - 1))
    jblk = lax.broadcasted_iota(jnp.int32, (GQ * tq, n_cmp_pad), 1)
    p = _softmax_rows(s, jblk * CMP_STRIDE + (CMP_BLOCK - 1) <= tpos_c)
    o_cmp = jnp.dot(p.astype(bf16), vc_ref[...], preferred_element_type=f32)
    psum = p[0:tq]
    for g in range(1, GQ):
        psum = psum + p[g * tq:(g + 1) * tq]

    imp = jnp.dot(psum, ov_ref[...], preferred_element_type=f32, precision=lax.Precision.HIGHEST)
    n_lanes = ov_ref.shape[1]
    n_slc = t_all // SLC_BLOCK
    nb = lax.broadcasted_iota(jnp.int32, (tq, n_lanes), 1)
    tpos = t0 + lax.broadcasted_iota(jnp.int32, (tq, n_lanes), 0)
    cur = tpos // SLC_BLOCK
    forced = (nb == 0) | (nb == cur) | (nb == cur - 1)
    valid = nb <= cur
    score = jnp.where(valid, jnp.where(forced, FORCE, imp), -FORCE)
    rank = jnp.zeros((tq, n_lanes), f32)
    for m in range(n_slc):
        col = score[:, m:m + 1]
        ahead = (col > score) | ((col == score) & (nb > m))
        rank = rank + jnp.where(ahead, 1.0, 0.0)
    sel = jnp.where((rank < float(min(TOP_K, n_slc))) & valid, 1.0, 0.0)
    sel_bf = sel.astype(bf16)

    def attend(qr, k, v, bias):
        sc = _dot_nt(qr, k) * SCALE + bias
        e = jnp.exp(sc - jnp.max(sc, axis=-1, keepdims=True))
        inv = 1.0 / jnp.sum(e, axis=-1, keepdims=True)
        return jnp.dot(e.astype(bf16), v, preferred_element_type=f32) * inv

    ext = min(SLC_KEY_EXT, t_all)
    for v in range(t_all // ext):
        nk = (v + 1) * ext

        @pl.when(t0 // ext == v)
        def _(nk=nk):
            allowed = jnp.dot(sel_bf, ex_ref[:, 0:nk], preferred_element_type=f32)
            kpos = lax.broadcasted_iota(jnp.int32, (tq, nk), 1)
            tq_pos = t0 + lax.broadcasted_iota(jnp.int32, (tq, nk), 0)
            bias_s = jnp.where((allowed > 0.5) & (kpos <= tq_pos), 0.0, NEG)
            ks = ks_ref[0:nk, :]
            vs = vs_ref[0:nk, :]
            for g in range(GQ):
                oslc_ref[g * tq:(g + 1) * tq, :] = attend(qr_ref[:, g * HEAD_DIM:(g + 1) * HEAD_DIM], ks, vs, bias_s)

    span = min(WINDOW + tq, t_all)
    w0 = pl.multiple_of(jnp.maximum(t0 + tq - span, 0), tq)
    kwin = kw_ref[pl.ds(w0, span), :]
    vwin = vw_ref[pl.ds(w0, span), :]
    wpos = w0 + lax.broadcasted_iota(jnp.int32, (tq, span), 1)
    wdiff = t0 + lax.broadcasted_iota(jnp.int32, (tq, span), 0) - wpos
    bias_w = jnp.where((wdiff >= 0) & (wdiff <= WINDOW), 0.0, NEG)

    gate = jax.nn.sigmoid(zg_ref[...])
    for g in range(GQ):
        hs = slice(g * HEAD_DIM, (g + 1) * HEAD_DIM)
        o_win = attend(qr_ref[:, hs], kwin, vwin, bias_w)
        o = (gate[:, 3 * g:3 * g + 1] * o_cmp[g * tq:(g + 1) * tq]
             + gate[:, 3 * g + 1:3 * g + 2] * oslc_ref[g * tq:(g + 1) * tq, :] + gate[:, 3 * g + 2:3 * g + 3] * o_win)
        o_ref[:, hs] = o.astype(o_ref.dtype)


def nsa_attention(qn, qr, kcmp, vcmp, ks, vs, kw, vw, zg):
    b, t, _ = qn.shape
    tq = min(t, WIN_QB)
    n_cmp_pad = kcmp.shape[1]
    n_slc = t // SLC_BLOCK
    starts = jnp.arange(n_cmp_pad, dtype=jnp.int32) * CMP_STRIDE
    blk_start = jnp.arange(LANE, dtype=jnp.int32) * SLC_BLOCK
    overlap = ((starts[:, None] < blk_start[None, :] + SLC_BLOCK) & (starts[:, None] + CMP_BLOCK > blk_start[None, :])
               & (jnp.arange(LANE)[None, :] < n_slc) & (starts[:, None] + CMP_BLOCK <= t)).astype(jnp.float32)
    expand = (jnp.arange(LANE, dtype=jnp.int32)[:, None] == (jnp.arange(t, dtype=jnp.int32) // SLC_BLOCK)[None, :]).astype(jnp.bfloat16)
    qspec = pl.BlockSpec((None, tq, GQ * HEAD_DIM), lambda bi, hi, qi: (bi, qi, hi))
    kvspec = lambda rows: pl.BlockSpec((None, rows, HEAD_DIM), lambda bi, hi, qi: (bi, 0, hi))
    const2 = lambda bi, hi, qi: (0, 0)
    return pl.pallas_call(
        _nsa_attn_kernel,
        grid=(b, KV_HEADS, t // tq),
        in_specs=[qspec, qspec, kvspec(n_cmp_pad), kvspec(n_cmp_pad), kvspec(t), kvspec(t), kvspec(t), kvspec(t),
                  pl.BlockSpec((None, tq, LANE), lambda bi, hi, qi: (bi, qi, hi)),
                  pl.BlockSpec((n_cmp_pad, LANE), const2), pl.BlockSpec((LANE, t), const2)],
        out_specs=qspec,
        out_shape=jax.ShapeDtypeStruct((b, t, D_C), jnp.bfloat16),
        scratch_shapes=[pltpu.VMEM((GQ * tq, HEAD_DIM), jnp.float32)],
        compiler_params=_cparams("parallel", "parallel", "arbitrary"),
        name="nsa_attention",
    )(qn, qr, kcmp, vcmp, ks, vs, kw, vw, zg, overlap, expand)


def _rope_tables(pos):
    half = ROPE_DIM // 2
    inv_freq = ROPE_THETA ** (-jnp.arange(half, dtype=jnp.float32) / half)
    ang = pos.astype(jnp.float32)[:, None] * inv_freq[None, :]
    cos, sin = jnp.cos(ang), jnp.sin(ang)
    t = pos.shape[0]
    cosf = jnp.concatenate([cos, cos, jnp.ones((t, HEAD_DIM - ROPE_DIM), jnp.float32)], axis=1)
    sinf = jnp.concatenate([-sin, sin, jnp.zeros((t, HEAD_DIM - ROPE_DIM), jnp.float32)], axis=1)
    return cosf, sinf


def _rmsnorm(x, g):
    xf = x.astype(jnp.float32)
    y = xf * lax.rsqrt(jnp.mean(xf * xf, axis=-1, keepdims=True) + EPS)
    return (y * g.astype(jnp.float32)).astype(x.dtype)


def _partial_rope(x, pos):
    half = ROPE_DIM // 2
    inv_freq = ROPE_THETA ** (-jnp.arange(half, dtype=jnp.float32) / half)
    ang = pos.astype(jnp.float32)[:, None] * inv_freq[None, :]
    cos = jnp.cos(ang)[None, :, None, :]
    sin = jnp.sin(ang)[None, :, None, :]
    xr = x[..., :ROPE_DIM].astype(jnp.float32)
    x1, x2 = xr[..., :half], xr[..., half:]
    rot = jnp.concatenate([x1 * cos - x2 * sin, x2 * cos + x1 * sin], axis=-1)
    return jnp.concatenate([rot.astype(x.dtype), x[..., ROPE_DIM:]], axis=-1)


def _masked_softmax(s, mask):
    s = jnp.where(mask, s, NEG)
    m = jnp.max(s, axis=-1, keepdims=True)
    e = jnp.where(mask, jnp.exp(s - m), 0.0)
    return e / jnp.maximum(jnp.sum(e, axis=-1, keepdims=True), 1e-30)


def _gqa_attend(q, k, v, mask):
    s = jnp.einsum('...qkgd,...skd->...qkgs', q, k).astype(jnp.float32) * SCALE
    p = _masked_softmax(s, mask)
    return jnp.einsum('...qkgs,...skd->...qkgd', p.astype(v.dtype), v), p


def _select_attend(q, sel, ok, pos, kg, vg):
    s = jnp.einsum('btkgd,btknsd->btkgns', q, kg).astype(jnp.float32) * SCALE
    tok = sel[..., None] * SLC_BLOCK + jnp.arange(SLC_BLOCK, dtype=sel.dtype)
    mask = (tok <= pos[None, :, None, None, None]) & ok[..., None]
    shp = s.shape
    p = _masked_softmax(s.reshape(*shp[:4], -1), mask.reshape(*mask.shape[:3], 1, -1)).reshape(shp)
    return jnp.einsum('btkgns,btknsd->btkgd', p.astype(vg.dtype), vg)


def _compress_paged_kernel(pt_ref, kcur_ref, knext_ref, vcur_ref, vnext_ref, wk_ref, wv_ref, kg_ref, pool_ref, pools_ref,
                           kc_ref, vc_ref):
    del pt_ref
    hp = lax.Precision.HIGHEST

    def weighted(cur_ref, next_ref, w_ref):
        x = cur_ref[...]
        return jnp.concatenate([x, x, next_ref[...]], axis=0) * w_ref[...]

    xw = jnp.concatenate([weighted(kcur_ref, knext_ref, wk_ref), weighted(vcur_ref, vnext_ref, wv_ref)], axis=1)
    pooled = jnp.dot(jnp.concatenate([pool_ref[...], pools_ref[...]], axis=1), xw, preferred_element_type=jnp.float32, precision=hp)
    kc = pooled[:, 0:KVD]
    vc_ref[...] = pooled[:, KVD:2 * KVD]
    for h in range(KV_HEADS):
        hs = slice(h * HEAD_DIM, (h + 1) * HEAD_DIM)
        kc_ref[:, hs] = _head_rmsnorm(kc[:, hs], kg_ref[0:1, :])


def nsa_compress_paged(pool_k, pool_v, layer, page_table, cmp_wk, cmp_wv, k_norm):
    depth, npool = pool_k.shape[:2]
    b, npages = page_table.shape
    rpp = PAGE_SIZE // CMP_STRIDE
    half = CMP_BLOCK // 2
    r1 = jnp.arange(PAGE_SIZE, dtype=jnp.int32)
    r2 = jnp.arange(2 * PAGE_SIZE, dtype=jnp.int32)
    m = jnp.arange(rpp, dtype=jnp.int32)[:, None]

    def wtile(w):
        w2 = w.transpose(1, 0, 2).reshape(CMP_BLOCK, KVD)
        return jnp.concatenate([w2[r1 % half], w2[half + r2 % half]], axis=0)

    pool = (r1[None, :] // CMP_STRIDE == m).astype(jnp.float32)
    pools = (r2[None, :] // CMP_STRIDE == m + 1).astype(jnp.float32)
    cur = pl.BlockSpec((None, None, PAGE_SIZE, KVD), lambda bi, pi, pt: (layer, pt[bi, pi], 0, 0))
    nxt = pl.BlockSpec((None, None, PAGE_SIZE, KVD), lambda bi, pi, pt: (layer, pt[bi, jnp.minimum(pi + 1, npages - 1)], 0, 0))
    const = lambda bi, pi, pt: (0, 0)
    out = pl.BlockSpec((None, rpp, KVD), lambda bi, pi, pt: (bi, pi, 0))
    grid_spec = pltpu.PrefetchScalarGridSpec(
        num_scalar_prefetch=1, grid=(b, npages),
        in_specs=[cur, nxt, cur, nxt, pl.BlockSpec((3 * PAGE_SIZE, KVD), const), pl.BlockSpec((3 * PAGE_SIZE, KVD), const),
                  pl.BlockSpec((3, HEAD_DIM), const), pl.BlockSpec((rpp, PAGE_SIZE), const), pl.BlockSpec((rpp, 2 * PAGE_SIZE), const)],
        out_specs=[out, out])
    pk = pool_k.reshape(depth, npool, PAGE_SIZE, KVD)
    pv = pool_v.reshape(depth, npool, PAGE_SIZE, KVD)
    return pl.pallas_call(
        _compress_paged_kernel,
        grid_spec=grid_spec,
        out_shape=[jax.ShapeDtypeStruct((b, npages * rpp, KVD), jnp.float32)] * 2,
        compiler_params=_cparams("parallel", "arbitrary"),
        name="nsa_compress_paged",
    )(page_table, pk, pk, pv, pv, wtile(cmp_wk), wtile(cmp_wv), k_norm, pool, pools)


def _nsa_compress_select(qn, qr, ks_new, vs_new, pos, past, lw):
    B_, T = qn.shape[:2]
    page_table = past['page_table']
    past_len = page_table.shape[1] * PAGE_SIZE
    L = past_len + T
    n_cmp = (L - CMP_BLOCK) // CMP_STRIDE + 1
    assert T == 1 and past_len % SLC_BLOCK == 0 and (n_cmp - 1) * CMP_STRIDE + CMP_BLOCK <= past_len
    starts = jnp.arange(n_cmp, dtype=jnp.int32) * CMP_STRIDE
    kcmp, vcmp = nsa_compress_paged(past['pool_kc'], past['pool_vc'], past['layer'], page_table, lw['nsa_cmp_wk'], lw['nsa_cmp_wv'], lw['nsa_k_norm'])
    k_cmp = kcmp[:, :n_cmp].reshape(B_, n_cmp, KV_HEADS, HEAD_DIM)
    v_cmp = vcmp[:, :n_cmp].reshape(B_, n_cmp, KV_HEADS, HEAD_DIM)
    mask_c = (starts + CMP_BLOCK - 1)[None, :] <= pos[:, None]
    o_cmp, p_cmp = _gqa_attend(qn, k_cmp, v_cmp, mask_c[None, :, None, None, :])
    n_slc = -(-L // SLC_BLOCK)
    blk = jnp.arange(n_slc, dtype=jnp.int32)
    blk_start = blk * SLC_BLOCK
    overlap = ((starts[:, None] < blk_start[None, :] + SLC_BLOCK) & (starts[:, None] + CMP_BLOCK > blk_start[None, :])).astype(jnp.float32)
    imp = jnp.einsum('btkgj,jn->btkn', p_cmp, overlap)
    cur = pos // SLC_BLOCK
    forced = (blk[None, :] == 0) | (blk[None, :] == cur[:, None]) | (blk[None, :] == cur[:, None] - 1)
    valid = blk[None, :] <= cur[:, None]
    score = jnp.where(forced[None, :, None, :], FORCE, imp)
    score = jnp.where(valid[None, :, None, :], score, -FORCE)
    _, sel = lax.top_k(score, min(TOP_K, n_slc))
    sel_ok = sel <= cur[None, :, None, None]
    n_past_blk = past_len // SLC_BLOCK
    per_page = PAGE_SIZE // SLC_BLOCK
    selc = jnp.minimum(sel, n_past_blk - 1)
    page = page_table[jnp.arange(B_)[:, None, None, None], selc // per_page]
    half = selc % per_page
    hi = jnp.arange(KV_HEADS)[None, None, :, None]

    def blocks(pool, new):
        p6 = pool.reshape(pool.shape[0], pool.shape[1], per_page, SLC_BLOCK, KV_HEADS, HEAD_DIM)
        g = p6[jnp.full_like(page, past['layer']), page, half, :, hi, :]
        newblk = jnp.pad(new.transpose(0, 2, 1, 3), ((0, 0), (0, 0), (0, SLC_BLOCK - T), (0, 0)))[:, None, :, None]
        return jnp.where((sel == n_past_blk)[..., None, None], newblk, g)

    o_slc = _select_attend(qr, sel, sel_ok, pos, blocks(past['pool_ks'], ks_new), blocks(past['pool_vs'], vs_new))
    return o_cmp, o_slc


def _nsa_mixer(z_q, z_kc, z_vc, z_ks, z_vs, z_kw, z_vw, z_g, pos, past, lw):
    B_, T = z_q.shape[:2]
    q = z_q.reshape(B_, T, N_HEADS, HEAD_DIM)
    qn = _rmsnorm(q, lw['nsa_q_norm'])
    qr = _partial_rope(qn, pos).reshape(B_, T, KV_HEADS, GQ, HEAD_DIM)
    qn = qn.reshape(B_, T, KV_HEADS, GQ, HEAD_DIM)
    kvshape = (B_, T, KV_HEADS, HEAD_DIM)
    kc_new = z_kc.reshape(kvshape)
    vc_new = z_vc.reshape(kvshape)
    ks_new = _partial_rope(_rmsnorm(z_ks.reshape(kvshape), lw['nsa_k_norm'][1]), pos)
    vs_new = z_vs.reshape(kvshape)
    kw_new = _partial_rope(_rmsnorm(z_kw.reshape(kvshape), lw['nsa_k_norm'][2]), pos)
    vw_new = z_vw.reshape(kvshape)
    wb = past['win_k'].shape[1]
    kw_all = jnp.concatenate([past['win_k'], kw_new], axis=1)
    vw_all = jnp.concatenate([past['win_v'], vw_new], axis=1)
    kpos = pos[0] - wb + jnp.arange(wb + T, dtype=jnp.int32)
    diff = pos[:, None] - kpos[None, :]
    mask = (diff >= 0) & (diff <= WINDOW)
    o_win, _ = _gqa_attend(qr, kw_all, vw_all, mask[None, :, None, None, :])
    win_k_new = kw_all[:, -wb:]
    win_v_new = vw_all[:, -wb:]
    o_cmp, o_slc = _nsa_compress_select(qn, qr, ks_new, vs_new, pos, past, lw)
    g = jax.nn.sigmoid(z_g.astype(jnp.float32)).reshape(B_, T, KV_HEADS, GQ, 3).astype(z_q.dtype)
    o = g[..., 0:1] * o_cmp + g[..., 1:2] * o_slc + g[..., 2:3] * o_win
    return o.reshape(B_, T, D_C), (kc_new, vc_new, ks_new, vs_new, win_k_new, win_v_new)


def _ffn(x2, g, wg, wu, wd):
    hn = rmsnorm_rows(x2, g)
    act = swiglu_up(hn, wg, wu)
    return mm_residual(act, wd, x2, 0.5)


def _pad_rows(x2, mult=16):
    m = x2.shape[0]
    mp = -(-m // mult) * mult
    return x2 if mp == m else jnp.pad(x2, ((0, mp - m), (0, 0)))


def _prompt_mixers(z3, zg3, pos, lw, wb):
    B_, T, _ = z3.shape
    y_a, tail = conv_prompt(z3, jnp.zeros((B_, CONV_HIST, D_A), jnp.float32), lw['conv_dw'], lw['conv_dw_bias'],
                            lw['conv_ln_g'], lw['conv_ln_b'])
    conv_new = tail[:, CONV_HIST - (CONV_WIDTH - 1):]
    y_b, s_last = s5_prompt(z3, jnp.zeros((B_, 1, 2 * N_STATE), jnp.float32), _s5_params(lw), lw['ssm_d'],
                            wb['ssm_w_glu'], lw['ssm_b_glu'])
    s_re = s_last[:, 0, :N_STATE].reshape(B_, SSM_GROUPS, SSM_STATE)
    s_im = s_last[:, 0, N_STATE:].reshape(B_, SSM_GROUPS, SSM_STATE)
    cosf, sinf = _rope_tables(pos)
    qn, qr, kc, vc, ks, vs, kw, vw, bks, bvs, bkw, bvw = nsa_prep(z3, cosf, sinf, lw['nsa_q_norm'], lw['nsa_k_norm'])
    kcmp, vcmp = nsa_compress(kc, vc, lw['nsa_cmp_wk'], lw['nsa_cmp_wv'], lw['nsa_k_norm'])
    y_c = nsa_attention(qn, qr, kcmp, vcmp, bks, bvs, bkw, bvw, zg3)
    kvshape = (B_, T, KV_HEADS, HEAD_DIM)
    keep = min(WINDOW, T)
    kv_new = (kc.reshape(kvshape), vc.reshape(kvshape), ks.reshape(kvshape), vs.reshape(kvshape),
              kw[:, -keep:].reshape(B_, keep, KV_HEADS, HEAD_DIM), vw[:, -keep:].reshape(B_, keep, KV_HEADS, HEAD_DIM))
    return y_a, y_b, y_c, kv_new + (conv_new, s_re, s_im)


def _sample_mixers(z2, z_g, pos, past, lw, wb, B_, T):
    assert T == 1
    m = B_ * T
    extra = z2.shape[0] - m
    hist = jnp.pad(past['conv'].transpose(1, 0, 2), ((0, 0), (0, extra), (0, 0)))
    y_a, hist_new = conv_step(z2[:, :D_A], z2[:, D_A:2 * D_A], hist, lw['conv_dw'], lw['conv_dw_bias'], lw['conv_ln_g'], lw['conv_ln_b'])
    conv_new = hist_new[:, :m].transpose(1, 0, 2)
    pad_state = lambda s: jnp.pad(s.reshape(m, N_STATE), ((0, extra), (0, 0)))
    y_b, s_re, s_im = s5_step(z2[:, 2 * D_A:2 * D_A + D_B], pad_state(past['s_re']), pad_state(past['s_im']), _s5_params(lw),
                              lw['ssm_d'], wb['ssm_w_glu'], lw['ssm_b_glu'])
    s_re = s_re[:m].reshape(B_, SSM_GROUPS, SSM_STATE)
    s_im = s_im[:m].reshape(B_, SSM_GROUPS, SSM_STATE)
    z = z2[:m].reshape(B_, T, D_MAIN)
    offs = []
    acc = Q_COL0
    for n in (D_C, KVD, KVD, KVD, KVD, KVD):
        acc += n
        offs.append(acc)
    z_q, z_kc, z_vc, z_ks, z_vs, z_kw, z_vw = jnp.split(z[..., Q_COL0:], [o - Q_COL0 for o in offs], axis=-1)
    y_c, kv_new = _nsa_mixer(z_q, z_kc, z_vc, z_ks, z_vs, z_kw, z_vw, z_g, pos, past, lw)
    return y_a[:m], y_b[:m], y_c, tuple(kv_new) + (conv_new, s_re, s_im)


def _layer_forward(x, pos, past, lw, wb):
    B_, T, _ = x.shape
    m = B_ * T
    x2 = _pad_rows(x.reshape(m, D_MODEL))
    h2 = _ffn(x2, lw['ffn1_norm'], wb['ffn1_w_gate'], wb['ffn1_w_up'], wb['ffn1_w_down'])
    u2 = rmsnorm_rows(h2, lw['mix_norm'])
    z = mm_bias(u2, wb['w_in_main'], jnp.zeros((D_MAIN,), jnp.float32))
    zg = mm_bias(u2, wb['w_in_gate'], jnp.zeros((KV_HEADS * LANE,), jnp.float32))
    if past is None:
        y_a, y_b, y_c, new_state = _prompt_mixers(z.reshape(B_, T, D_MAIN), zg.reshape(B_, T, KV_HEADS * LANE), pos, lw, wb)
    else:
        z_g = zg[:m].reshape(B_, T, KV_HEADS, LANE)[..., :3 * GQ].reshape(B_, T, N_GATE)
        y_a, y_b, y_c, new_state = _sample_mixers(z, z_g, pos, past, lw, wb, B_, T)
    rows = lambda y: _pad_rows(y.reshape(m, -1)).astype(jnp.bfloat16)
    mixed = gated_merge(u2, rows(y_a), rows(y_b), rows(y_c), wb['merge_w_gate'], lw['merge_b_gate'],
                        wb['conv_w_out'], wb['ssm_w_out'], wb['nsa_w_out'])
    h2 = mm_residual(mixed, wb['w_out'], h2, 1.0)
    y2 = _ffn(h2, lw['ffn2_norm'], wb['ffn2_w_gate'], wb['ffn2_w_up'], wb['ffn2_w_down'])
    return y2[:m].reshape(B_, T, D_MODEL), new_state


def kernel(x_prompt, x_sample, cache_cmp_k, cache_cmp_v, cache_slc_k, cache_slc_v, cache_win_k, cache_win_v, state_conv, state_ssm_re, state_ssm_im, page_table, ffn1_norm, ffn1_w_gate, ffn1_w_up, ffn1_w_down, mix_norm, w_in, conv_dw, conv_dw_bias, conv_ln_g, conv_ln_b, conv_w_out, ssm_lambda_re, ssm_lambda_im, ssm_log_dt, ssm_b_re, ssm_b_im, ssm_c_re, ssm_c_im, ssm_d, ssm_w_glu, ssm_b_glu, ssm_w_out, nsa_q_norm, nsa_k_norm, nsa_cmp_wk, nsa_cmp_wv, nsa_w_out, merge_w_gate, merge_b_gate, w_out, ffn2_norm, ffn2_w_gate, ffn2_w_up, ffn2_w_down):
    past_len = page_table.shape[1] * PAGE_SIZE
    pos_p = jnp.arange(x_prompt.shape[1], dtype=jnp.int32)
    pos_s = past_len + jnp.arange(x_sample.shape[1], dtype=jnp.int32)
    hp, hs = x_prompt, x_sample
    st_p, st_s = [], []
    bf = jnp.bfloat16
    for l in range(DEPTH):
        lw = dict(ffn1_norm=ffn1_norm[l], mix_norm=mix_norm[l],
                  conv_dw=conv_dw[l], conv_dw_bias=conv_dw_bias[l], conv_ln_g=conv_ln_g[l], conv_ln_b=conv_ln_b[l],
                  ssm_lambda_re=ssm_lambda_re[l], ssm_lambda_im=ssm_lambda_im[l], ssm_log_dt=ssm_log_dt[l],
                  ssm_b_re=ssm_b_re[l], ssm_b_im=ssm_b_im[l], ssm_c_re=ssm_c_re[l], ssm_c_im=ssm_c_im[l],
                  ssm_d=ssm_d[l], ssm_b_glu=ssm_b_glu[l],
                  nsa_q_norm=nsa_q_norm[l], nsa_k_norm=nsa_k_norm[l], nsa_cmp_wk=nsa_cmp_wk[l], nsa_cmp_wv=nsa_cmp_wv[l],
                  merge_b_gate=merge_b_gate[l], ffn2_norm=ffn2_norm[l])
        wb = dict(ffn1_w_gate=ffn1_w_gate[l].astype(bf), ffn1_w_up=ffn1_w_up[l].astype(bf), ffn1_w_down=ffn1_w_down[l].astype(bf),
                  w_in_main=w_in[l][:, :D_MAIN].astype(bf),
                  w_in_gate=jnp.pad(w_in[l][:, D_MAIN:].reshape(D_MODEL, KV_HEADS, 3 * GQ), ((0, 0), (0, 0), (0, LANE - 3 * GQ))).reshape(D_MODEL, KV_HEADS * LANE).astype(bf),
                  conv_w_out=conv_w_out[l].astype(bf), ssm_w_glu=ssm_w_glu[l].astype(bf), ssm_w_out=ssm_w_out[l].astype(bf),
                  nsa_w_out=nsa_w_out[l].astype(bf), merge_w_gate=merge_w_gate[l].astype(bf), w_out=w_out[l].astype(bf),
                  ffn2_w_gate=ffn2_w_gate[l].astype(bf), ffn2_w_up=ffn2_w_up[l].astype(bf), ffn2_w_down=ffn2_w_down[l].astype(bf))
        hp, new_p = _layer_forward(hp, pos_p, None, lw, wb)
        past = dict(pool_kc=cache_cmp_k, pool_vc=cache_cmp_v, pool_ks=cache_slc_k, pool_vs=cache_slc_v, layer=l,
                    page_table=page_table, win_k=cache_win_k[l], win_v=cache_win_v[l], conv=state_conv[l],
                    s_re=state_ssm_re[l], s_im=state_ssm_im[l])
        hs, new_s = _layer_forward(hs, pos_s, past, lw, wb)
        st_p.append(new_p)
        st_s.append(new_s)
    P = [jnp.stack([st[i] for st in st_p], axis=0) for i in range(9)]
    S = [jnp.stack([st[i] for st in st_s], axis=0) for i in range(9)]
    return (hp, hs, P[0], P[1], P[2], P[3], P[4], P[5], P[6], P[7], P[8], S[0], S[1], S[2], S[3], S[4], S[5], S[6], S[7], S[8])
```
